```python
import math
import jax
import jax.numpy as jnp
from jax import lax
import numpy as np

D_MODEL = 1024
BATCH = 4
SEQ = 8192
DEPTH = 2

CHUNK = 64
Q_BLOCK = 128
H_A = 8
Q_LORA = 256
KV_LORA = 128
NOPE_DIM = 64
ROPE_DIM = 32
V_DIM_A = 64
ROPE_BASE = 10000.0
H_B = 4
DH_B = 64
T5_BUCKETS = 32
T5_MAX_DIST = 128
H_C = 8
DH_C = 64
H_D = 8
DH_D = 64
BAND_CHUNKS = 8
REL_CLIP = 128
D_FF = 2816
CONV_W = 3
EPS_LN = 1e-5
EPS_RMS = 1e-6
DEEPNORM_ALPHA = (2 * DEPTH) ** 0.25
DEEPNORM_BETA = (8 * DEPTH) ** -0.25
NEG_INF = -1e30

kernel_name = 'hybrid_mla_diff_fox_chunk_convffn'


def layer_norm(x, g, b):
    xf = x.astype(jnp.float32)
    mu = jnp.mean(xf, axis=-1, keepdims=True)
    var = jnp.mean(jnp.square(xf - mu), axis=-1, keepdims=True)
    return ((xf - mu) * lax.rsqrt(var + EPS_LN) * g + b).astype(x.dtype)


def rms_norm(x, g):
    xf = x.astype(jnp.float32)
    ms = jnp.mean(jnp.square(xf), axis=-1, keepdims=True)
    return (xf * lax.rsqrt(ms + EPS_RMS) * g).astype(x.dtype)


def split_cols(h, sizes):
    return jnp.split(h, np.cumsum(sizes)[:-1].tolist(), axis=-1)


def rope_cos_sin(seq_len):
    half = ROPE_DIM // 2
    inv = jnp.power(ROPE_BASE, -jnp.arange(half, dtype=jnp.float32) / half)
    ang = jnp.arange(seq_len, dtype=jnp.float32)[:, None] * inv[None, :]
    return jnp.cos(ang), jnp.sin(ang)


def apply_rope(x, cos, sin):
    half = x.shape[-1] // 2
    x1, x2 = x[..., :half], x[..., half:]
    cos, sin = cos.astype(x.dtype), sin.astype(x.dtype)
    return jnp.concatenate([x1 * cos - x2 * sin, x1 * sin + x2 * cos], axis=-1)


def t5_bucket(rel):
    half = T5_BUCKETS // 2
    max_exact = half // 2
    n = jnp.abs(rel)
    large = max_exact + (jnp.log(jnp.maximum(n, 1).astype(jnp.float32) / max_exact)
                         / math.log(T5_MAX_DIST / max_exact) * (half - max_exact)).astype(jnp.int32)
    large = jnp.minimum(large, half - 1)
    return jnp.where(rel > 0, half, 0) + jnp.where(n < max_exact, n, large)


def chunk_causal_mask(q0, key_idx):
    q_idx = q0 + jnp.arange(Q_BLOCK)
    return (key_idx[None, :] // CHUNK) <= (q_idx[:, None] // CHUNK)


def sweep_blocks(block_fn, n_blocks):
    out = lax.map(block_fn, jnp.arange(n_blocks))
    nb, b, l, h, d = out.shape
    return jnp.moveaxis(out, 0, 1).reshape(b, nb * l, h, d)


def modulate(x, cond, w, b):
    shift, scale, gate = jnp.split(cond @ w + b, 3, axis=-1)
    return x * (1.0 + scale[:, None, :]) + shift[:, None, :], (1.0 + gate)[:, None, :]


def mixer_ab(u, w_in, q_norm, w_uq, kv_norm, w_ukv, lq1, lk1, lq2, lk2, sub_g, t5_table, w_out, layer_idx):
    b, s_len, _ = u.shape
    wb = H_B * 2 * DH_B
    cq, ckv, kr, qb, kb, vb = split_cols(u @ w_in, (Q_LORA, KV_LORA, ROPE_DIM, wb, wb, wb))
    key_idx = jnp.arange(s_len)
    n_blocks = s_len // Q_BLOCK

    cos, sin = rope_cos_sin(s_len)
    qa = (rms_norm(cq, q_norm) @ w_uq).reshape(b, s_len, H_A, NOPE_DIM + ROPE_DIM)
    q_nope = qa[..., :NOPE_DIM]
    q_rope = apply_rope(qa[..., NOPE_DIM:], cos[:, None, :], sin[:, None, :])
    kv = (rms_norm(ckv, kv_norm) @ w_ukv).reshape(b, s_len, H_A, NOPE_DIM + V_DIM_A)
    k_nope, v_a = kv[..., :NOPE_DIM], kv[..., NOPE_DIM:]
    k_rope = apply_rope(kr, cos, sin)
    scale_a = (NOPE_DIM + ROPE_DIM) ** -0.5

    def mla_block(i):
        q0 = i * Q_BLOCK
        qn = lax.dynamic_slice_in_dim(q_nope, q0, Q_BLOCK, axis=1)
        qr = lax.dynamic_slice_in_dim(q_rope, q0, Q_BLOCK, axis=1)
        logits = (jnp.einsum('bqhd,bkhd->bhqk', qn, k_nope, preferred_element_type=jnp.float32)
                  + jnp.einsum('bqhr,bkr->bhqk', qr, k_rope, preferred_element_type=jnp.float32)) * scale_a
        logits = jnp.where(chunk_causal_mask(q0, key_idx), logits, NEG_INF)
        p = jax.nn.softmax(logits, axis=-1).astype(v_a.dtype)
        return jnp.einsum('bhqk,bkhd->bqhd', p, v_a)

    o_a = sweep_blocks(mla_block, n_blocks)

    qd = qb.reshape(b, s_len, H_B, 2, DH_B)
    kd = kb.reshape(b, s_len, H_B, 2, DH_B)
    vd = vb.reshape(b, s_len, H_B, 2 * DH_B)
    lam_init = 0.8 - 0.6 * math.exp(-0.3 * layer_idx)
    lam = (jnp.exp(jnp.sum(lq1.astype(jnp.float32) * lk1.astype(jnp.float32)))
           - jnp.exp(jnp.sum(lq2.astype(jnp.float32) * lk2.astype(jnp.float32))) + lam_init)
    scale_b = DH_B ** -0.5

    def diff_block(i):
        q0 = i * Q_BLOCK
        qq = lax.dynamic_slice_in_dim(qd, q0, Q_BLOCK, axis=1)
        q_idx = q0 + jnp.arange(Q_BLOCK)
        bias = jnp.take(t5_table, t5_bucket(key_idx[None, :] - q_idx[:, None]), axis=1)
        logits = jnp.einsum('bqhcd,bkhcd->bchqk', qq, kd, preferred_element_type=jnp.float32) * scale_b + bias
        logits = jnp.where(chunk_causal_mask(q0, key_idx), logits, NEG_INF)
        p = jax.nn.softmax(logits, axis=-1)
        attn = (p[:, 0] - lam * p[:, 1]).astype(vd.dtype)
        return jnp.einsum('bhqk,bkhd->bqhd', attn, vd)

    o_b = sweep_blocks(diff_block, n_blocks)
    o_b = rms_norm(o_b, sub_g) * (1.0 - lam_init)

    merged = jnp.concatenate([o_a.reshape(b, s_len, H_A * V_DIM_A), o_b.reshape(b, s_len, wb)], axis=-1)
    return merged @ w_out


def mixer_cd(u, w_in, b_f, rel_table, w_out):
    b, s_len, _ = u.shape
    wc, wd = H_C * DH_C, H_D * DH_D
    qc, kc, vc, fl, qd, kd, vd = split_cols(u @ w_in, (wc, wc, wc, H_C, wd, wd, wd))
    qc, kc, vc = (t.reshape(b, s_len, H_C, DH_C) for t in (qc, kc, vc))
    qd, kd, vd = (t.reshape(b, s_len, H_D, DH_D) for t in (qd, kd, vd))
    key_idx = jnp.arange(s_len)

    log_f = jax.nn.log_sigmoid(fl.astype(jnp.float32) + b_f)
    cum_f = jnp.moveaxis(jnp.cumsum(log_f, axis=1), 1, 2)
    scale_c = DH_C ** -0.5

    def fox_block(i):
        q0 = i * Q_BLOCK
        qq = lax.dynamic_slice_in_dim(qc, q0, Q_BLOCK, axis=1)
        fq = lax.dynamic_slice_in_dim(cum_f, q0, Q_BLOCK, axis=2)
        q_idx = q0 + jnp.arange(Q_BLOCK)
        logits = (jnp.einsum('bqhd,bkhd->bhqk', qq, kc, preferred_element_type=jnp.float32) * scale_c
                  + fq[..., :, None] - cum_f[:, :, None, :])
        logits = jnp.where(key_idx[None, :] <= q_idx[:, None], logits, NEG_INF)
        p = jax.nn.softmax(logits, axis=-1).astype(vc.dtype)
        return jnp.einsum('bhqk,bkhd->bqhd', p, vc)

    o_c = sweep_blocks(fox_block, s_len // Q_BLOCK)

    lead = BAND_CHUNKS * CHUNK
    band = lead + CHUNK
    pad = ((0, 0), (lead, 0), (0, 0), (0, 0))
    kp, vp = jnp.pad(kd, pad), jnp.pad(vd, pad)
    kj = jnp.arange(band)
    rel = (lead + jnp.arange(CHUNK))[:, None] - kj[None, :]
    bias_d = jnp.take(rel_table, jnp.clip(rel, -REL_CLIP, REL_CLIP) + REL_CLIP, axis=1)
    scale_d = DH_D ** -0.5

    def chunk_block(ci):
        q0 = ci * CHUNK
        qq = lax.dynamic_slice_in_dim(qd, q0, CHUNK, axis=1)
        kk = lax.dynamic_slice_in_dim(kp, q0, band, axis=1)
        vv = lax.dynamic_slice_in_dim(vp, q0, band, axis=1)
        logits = jnp.einsum('bqhd,bkhd->bhqk', qq, kk, preferred_element_type=jnp.float32) * scale_d + bias_d
        logits = jnp.where((q0 - lead + kj) >= 0, logits, NEG_INF)
        p = jax.nn.softmax(logits, axis=-1).astype(vv.dtype)
        return jnp.einsum('bhqk,bkhd->bqhd', p, vv)

    o_d = sweep_blocks(chunk_block, s_len // CHUNK)

    merged = jnp.concatenate([o_c.reshape(b, s_len, wc), o_d.reshape(b, s_len, wd)], axis=-1)
    return merged @ w_out


def conv_ffn(u, w_gate, w_val, conv_w, conv_b, w_down):
    s_len = u.shape[1]
    g = u @ w_gate
    gp = jnp.pad(g, ((0, 0), (CONV_W - 1, 0), (0, 0)))
    g = sum(conv_w[j] * gp[:, j:j + s_len] for j in range(CONV_W)) + conv_b
    return (jax.nn.silu(g) * (u @ w_val)) @ w_down


def setup_inputs(seed: int = 0) -> dict:
    key = jax.random.key(seed)
    ks = iter(jax.random.split(key, 32))

    def nrm(shape, scale):
        return jax.random.normal(next(ks), shape, jnp.float32) * scale

    d = D_MODEL
    ne, no = (DEPTH + 1) // 2, DEPTH // 2
    in_ab = Q_LORA + KV_LORA + ROPE_DIM + 3 * H_B * 2 * DH_B
    in_cd = 3 * H_C * DH_C + H_C + 3 * H_D * DH_D
    mix_ab = H_A * V_DIM_A + H_B * 2 * DH_B
    mix_cd = H_C * DH_C + H_D * DH_D
    return {
        'x': nrm((BATCH, SEQ, d), 1.0),
        'c': nrm((BATCH, d), 1.0),
        'ada_w': nrm((DEPTH, 2, d, 3 * d), 0.1 * d ** -0.5),
        'ada_b': nrm((DEPTH, 2, 3 * d), 0.01),
        'ln_g': 1.0 + nrm((DEPTH, 2, d), 0.02),
        'ln_b': nrm((DEPTH, 2, d), 0.02),
        't5_table': nrm((H_B, T5_BUCKETS), 0.5),
        'ab_w_in': nrm((ne, d, in_ab), d ** -0.5),
        'mla_q_norm': 1.0 + nrm((ne, Q_LORA), 0.02),
        'mla_w_uq': nrm((ne, Q_LORA, H_A * (NOPE_DIM + ROPE_DIM)), Q_LORA ** -0.5),
        'mla_kv_norm': 1.0 + nrm((ne, KV_LORA), 0.02),
        'mla_w_ukv': nrm((ne, KV_LORA, H_A * (NOPE_DIM + V_DIM_A)), KV_LORA ** -0.5),
        'diff_lq1': nrm((ne, DH_B), 0.1),
        'diff_lk1': nrm((ne, DH_B), 0.1),
        'diff_lq2': nrm((ne, DH_B), 0.1),
        'diff_lk2': nrm((ne, DH_B), 0.1),
        'diff_sub_g': 1.0 + nrm((ne, 2 * DH_B), 0.02),
        'ab_w_out': nrm((ne, mix_ab, d), mix_ab ** -0.5 * DEEPNORM_BETA),
        'cd_w_in': nrm((no, d, in_cd), d ** -0.5),
        'fox_b_f': 3.0 + nrm((no, H_C), 0.5),
        'chunk_rel_table': nrm((no, H_D, 2 * REL_CLIP + 1), 0.5),
        'cd_w_out': nrm((no, mix_cd, d), mix_cd ** -0.5 * DEEPNORM_BETA),
        'ffn_w_gate': nrm((DEPTH, d, D_FF), d ** -0.5),
        'ffn_w_val': nrm((DEPTH, d, D_FF), d ** -0.5),
        'ffn_conv_w': nrm((DEPTH, CONV_W, D_FF), CONV_W ** -0.5),
        'ffn_conv_b': nrm((DEPTH, D_FF), 0.02),
        'ffn_w_down': nrm((DEPTH, D_FF, d), D_FF ** -0.5 * DEEPNORM_BETA),
    }


def reference(x, c, ada_w, ada_b, ln_g, ln_b, t5_table, ab_w_in, mla_q_norm, mla_w_uq, mla_kv_norm,
              mla_w_ukv, diff_lq1, diff_lk1, diff_lq2, diff_lk2, diff_sub_g, ab_w_out, cd_w_in, fox_b_f,
              chunk_rel_table, cd_w_out, ffn_w_gate, ffn_w_val, ffn_conv_w, ffn_conv_b, ffn_w_down):
    cond = jax.nn.silu(c)
    for i in range(DEPTH):
        u, gate = modulate(x, cond, ada_w[i, 0], ada_b[i, 0])
        if i % 2 == 0:
            e = i // 2
            y = mixer_ab(u, ab_w_in[e], mla_q_norm[e], mla_w_uq[e], mla_kv_norm[e], mla_w_ukv[e],
                         diff_lq1[e], diff_lk1[e], diff_lq2[e], diff_lk2[e], diff_sub_g[e], t5_table,
                         ab_w_out[e], i)
        else:
            o = i // 2
            y = mixer_cd(u, cd_w_in[o], fox_b_f[o], chunk_rel_table[o], cd_w_out[o])
        x = layer_norm(DEEPNORM_ALPHA * x + gate * y, ln_g[i, 0], ln_b[i, 0])
        u, gate = modulate(x, cond, ada_w[i, 1], ada_b[i, 1])
        y = conv_ffn(u, ffn_w_gate[i], ffn_w_val[i], ffn_conv_w[i], ffn_conv_b[i], ffn_w_down[i])
        x = layer_norm(DEEPNORM_ALPHA * x + gate * y, ln_g[i, 1], ln_b[i, 1])
    return x
```

```python
import functools
import math

import jax
import jax.numpy as jnp
from jax import lax
from jax.experimental import pallas as pl
from jax.experimental.pallas import tpu as pltpu

DEPTH = 2
CHUNK = 64
H_A, Q_LORA, KV_LORA, NOPE_DIM, ROPE_DIM, V_DIM_A = 8, 256, 128, 64, 32, 64
ROPE_BASE = 10000.0
H_B, DH_B, T5_BUCKETS, T5_MAX_DIST = 4, 64, 32, 128
H_C, DH_C = 8, 64
H_D, DH_D, BAND_CHUNKS, REL_CLIP = 8, 64, 8, 128
CONV_W = 3
EPS_LN = 1e-5
EPS_RMS = 1e-6
DEEPNORM_ALPHA = (2 * DEPTH) ** 0.25
NEG_INF = -1e30

LANES = 128
ATTN_TILE = 512
ROW_TILE = 512
FFN_ROW_TILE = 256
BAND_GROUP = 2 * CHUNK
BAND_LEAD = BAND_CHUNKS * CHUNK
BAND_WIN = BAND_LEAD + BAND_GROUP
VMEM_LIMIT = 56 * 1024 * 1024

BF16 = jnp.bfloat16
F32 = jnp.float32


def _cparams(*sem):
    return pltpu.CompilerParams(dimension_semantics=sem, vmem_limit_bytes=VMEM_LIMIT)


def _dot(a, b):
    return jnp.dot(a, b, preferred_element_type=F32)


def _dot_nt(a, b):
    return lax.dot_general(a, b, (((1,), (1,)), ((), ())), preferred_element_type=F32)


def _layer_norm(z, g, b):
    mu = jnp.mean(z, axis=-1, keepdims=True)
    zc = z - mu
    var = jnp.mean(zc * zc, axis=-1, keepdims=True)
    return zc * lax.rsqrt(var + EPS_LN) * g + b


def _rms_norm(z, g):
    ms = jnp.mean(z * z, axis=-1, keepdims=True)
    return z * lax.rsqrt(ms + EPS_RMS) * g


def _const_spec(shape):
    return pl.BlockSpec(shape, lambda *_: (0,) * len(shape))


def _adaln_kernel(c_ref, w_ref, b_ref, o_ref):
    c = c_ref[...]
    cond = c * jax.nn.sigmoid(c)
    y = jnp.dot(cond, w_ref[0], preferred_element_type=F32, precision=lax.Precision.HIGHEST)
    one = jnp.where(pl.program_id(1) > 0, 1.0, 0.0).astype(F32)
    o_ref[0] = y + b_ref[0] + one


def _adaln(c, ada_w, ada_b):
    bsz, d = c.shape
    n_sub = ada_w.shape[0] * ada_w.shape[1]
    w = ada_w.reshape(n_sub, d, 3 * d)
    b = ada_b.reshape(n_sub, 1, 3 * d)
    return pl.pallas_call(
        _adaln_kernel,
        grid=(n_sub, 3),
        in_specs=[
            pl.BlockSpec((bsz, d), lambda l, n: (0, 0)),
            pl.BlockSpec((1, d, d), lambda l, n: (l, 0, n)),
            pl.BlockSpec((1, 1, d), lambda l, n: (l, 0, n)),
        ],
        out_specs=pl.BlockSpec((1, bsz, d), lambda l, n: (l, 0, n)),
        out_shape=jax.ShapeDtypeStruct((n_sub, bsz, 3 * d), F32),
        compiler_params=_cparams("arbitrary", "arbitrary"),
        name="adaln",
    )(c, w, b)


def _modulate(x, mod, d):
    return x * mod[:, d:2 * d] + mod[:, :d], mod[:, 2 * d:]


def _proj_ab_kernel(x_ref, mod_ref, w_in_ref, qn_ref, kvn_ref, wq_ref, wk_ref, wv_ref,
                    cq_ref, sq_ref, ck_ref, sk_ref,
                    qa_ref, ka_ref, va_ref, qb_ref, kb_ref, vb_ref):
    d = x_ref.shape[-1]
    u, _ = _modulate(x_ref[0], mod_ref[0], d)
    h = _dot(u.astype(BF16), w_in_ref[...])
    o = 0
    cq = h[:, o:o + Q_LORA]; o += Q_LORA
    ckv = h[:, o:o + KV_LORA]; o += KV_LORA
    kr_a = h[:, o:o + LANES]; o += LANES
    kr_b = h[:, o:o + LANES]; o += LANES
    wb = H_B * LANES
    qb = h[:, o:o + wb]; o += wb
    kb = h[:, o:o + wb]; o += wb
    vb = h[:, o:o + wb]

    nq = _rms_norm(cq, qn_ref[...]).astype(BF16)
    q12 = _dot(nq, wq_ref[...])
    cq_t, sq_t = cq_ref[...], sq_ref[...]
    for hh in range(H_A):
        a = q12[:, hh * LANES:(hh + 1) * LANES]
        b = q12[:, (H_A + hh) * LANES:(H_A + hh + 1) * LANES]
        qa_ref[0, hh] = (a * cq_t + b * sq_t).astype(BF16)

    nkv = _rms_norm(ckv, kvn_ref[...]).astype(BF16)
    kn = _dot(nkv, wk_ref[...])
    vv = _dot(nkv, wv_ref[...])
    k_rope = kr_a * ck_ref[...] + kr_b * sk_ref[...]
    for hh in range(H_A):
        ka_ref[0, hh] = (kn[:, hh * LANES:(hh + 1) * LANES] + k_rope).astype(BF16)
    for p in range(H_A // 2):
        va_ref[0, p] = vv[:, p * LANES:(p + 1) * LANES].astype(BF16)
    scale_b = DH_B ** -0.5
    for hh in range(H_B):
        sl = slice(hh * LANES, (hh + 1) * LANES)
        qb_ref[0, hh] = (qb[:, sl] * scale_b).astype(BF16)
        kb_ref[0, hh] = kb[:, sl].astype(BF16)
        vb_ref[0, hh] = vb[:, sl].astype(BF16)


def _rope_tables(s_len):
    half = ROPE_DIM // 2
    inv = jnp.power(ROPE_BASE, -jnp.arange(half, dtype=F32) / half)
    ang = jnp.arange(s_len, dtype=F32)[:, None] * inv[None, :]
    cos, sin = jnp.cos(ang), jnp.sin(ang)
    zeros_pad = jnp.zeros((s_len, LANES - NOPE_DIM - ROPE_DIM), F32)
    c_rope = jnp.concatenate([cos, cos], axis=1)
    s_rope = jnp.concatenate([-sin, sin], axis=1)
    cq = jnp.concatenate([jnp.ones((s_len, NOPE_DIM), F32), c_rope, zeros_pad], axis=1)
    ck = jnp.concatenate([jnp.zeros((s_len, NOPE_DIM), F32), c_rope, zeros_pad], axis=1)
    sk = jnp.concatenate([jnp.zeros((s_len, NOPE_DIM), F32), s_rope, zeros_pad], axis=1)
    scale_a = (NOPE_DIM + ROPE_DIM) ** -0.5
    return cq * scale_a, sk * scale_a, ck, sk


def _swap_halves(w):
    half = w.shape[-1] // 2
    return jnp.concatenate([w[..., half:], w[..., :half]], axis=-1)


def _proj_ab_weights(w_in, w_uq, w_ukv):
    d = w_in.shape[0]
    wb = H_B * 2 * DH_B
    o = 0
    w_cq = w_in[:, o:o + Q_LORA]; o += Q_LORA
    w_ckv = w_in[:, o:o + KV_LORA]; o += KV_LORA
    w_kr = w_in[:, o:o + ROPE_DIM]; o += ROPE_DIM
    w_rest = w_in[:, o:o + 3 * wb]
    lead = jnp.zeros((d, NOPE_DIM), F32)
    tail = jnp.zeros((d, LANES - NOPE_DIM - ROPE_DIM), F32)
    w_kr_a = jnp.concatenate([lead, w_kr, tail], axis=1)
    w_kr_b = jnp.concatenate([lead, _swap_halves(w_kr), tail], axis=1)
    w_in_aug = jnp.concatenate([w_cq, w_ckv, w_kr_a, w_kr_b, w_rest], axis=1).astype(BF16)

    wq = w_uq.reshape(Q_LORA, H_A, NOPE_DIM + ROPE_DIM)
    wq_nope, wq_rope = wq[..., :NOPE_DIM], wq[..., NOPE_DIM:]
    zpad = jnp.zeros((Q_LORA, H_A, LANES - NOPE_DIM - ROPE_DIM), F32)
    wq1 = jnp.concatenate([wq_nope, wq_rope, zpad], axis=-1).reshape(Q_LORA, H_A * LANES)
    wq2 = jnp.concatenate([jnp.zeros_like(wq_nope), _swap_halves(wq_rope), zpad], axis=-1)
    wq12 = jnp.concatenate([wq1, wq2.reshape(Q_LORA, H_A * LANES)], axis=1).astype(BF16)

    wkv = w_ukv.reshape(KV_LORA, H_A, NOPE_DIM + V_DIM_A)
    wk = jnp.concatenate([wkv[..., :NOPE_DIM], jnp.zeros((KV_LORA, H_A, LANES - NOPE_DIM), F32)], axis=-1)
    wk = wk.reshape(KV_LORA, H_A * LANES).astype(BF16)
    wv = wkv[..., NOPE_DIM:].reshape(KV_LORA, H_A * V_DIM_A).astype(BF16)
    return w_in_aug, wq12, wk, wv


def _proj_ab(x, mod, w_in, q_norm, w_uq, kv_norm, w_ukv):
    bsz, s_len, d = x.shape
    tm = min(ROW_TILE, s_len)
    w_in_aug, wq12, wk, wv = _proj_ab_weights(w_in, w_uq, w_ukv)
    tabs = _rope_tables(s_len)
    row = lambda b, t: (b, t, 0)
    head = lambda b, t: (b, 0, t, 0)
    tab_spec = pl.BlockSpec((tm, LANES), lambda b, t: (t, 0))
    out = lambda h: jax.ShapeDtypeStruct((bsz, h, s_len, LANES), BF16)
    out_spec = lambda h: pl.BlockSpec((1, h, tm, LANES), head)
    return pl.pallas_call(
        _proj_ab_kernel,
        grid=(bsz, s_len // tm),
        in_specs=[
            pl.BlockSpec((1, tm, d), row),
            pl.BlockSpec((1, 1, 3 * d), lambda b, t: (b, 0, 0)),
            _const_spec(w_in_aug.shape),
            _const_spec((1, Q_LORA)), _const_spec((1, KV_LORA)),
            _const_spec(wq12.shape), _const_spec(wk.shape), _const_spec(wv.shape),
            tab_spec, tab_spec, tab_spec, tab_spec,
        ],
        out_specs=[out_spec(H_A), out_spec(H_A), out_spec(H_A // 2),
                   out_spec(H_B), out_spec(H_B), out_spec(H_B)],
        out_shape=[out(H_A), out(H_A), out(H_A // 2), out(H_B), out(H_B), out(H_B)],
        compiler_params=_cparams("parallel", "parallel"),
        name="proj_ab",
    )(x, mod, w_in_aug, q_norm.reshape(1, -1), kv_norm.reshape(1, -1), wq12, wk, wv, *tabs)


def _proj_cd_kernel(x_ref, mod_ref, w_ref, wf_ref, qc_ref, kc_ref, vc_ref, qd_ref, kd_ref, vd_ref, fl_ref):
    d = x_ref.shape[-1]
    u, _ = _modulate(x_ref[0], mod_ref[0], d)
    u = u.astype(BF16)
    h = _dot(u, w_ref[...])
    fl_ref[0] = _dot_nt(wf_ref[...], u)
    scale = DH_C ** -0.5
    n_pair = H_C // 2
    for k, (ref, sc) in enumerate(((qc_ref, scale), (kc_ref, None), (vc_ref, None),
                                   (qd_ref, scale), (kd_ref, None), (vd_ref, None))):
        for p in range(n_pair):
            blk = h[:, (k * n_pair + p) * LANES:(k * n_pair + p + 1) * LANES]
            if sc is not None:
                blk = blk * sc
            ref[0, p] = blk.astype(BF16)


def _proj_cd(x, mod, w_in):
    bsz, s_len, d = x.shape
    tm = min(ROW_TILE, s_len)
    wc, wd = H_C * DH_C, H_D * DH_D
    o = 3 * wc
    w_main = jnp.concatenate([w_in[:, :o], w_in[:, o + H_C:]], axis=1).astype(BF16)
    w_f = w_in[:, o:o + H_C].T.astype(BF16)
    n_pair = H_C // 2
    head = lambda b, t: (b, 0, t, 0)
    out = jax.ShapeDtypeStruct((bsz, n_pair, s_len, LANES), BF16)
    out_spec = pl.BlockSpec((1, n_pair, tm, LANES), head)
    return pl.pallas_call(
        _proj_cd_kernel,
        grid=(bsz, s_len // tm),
        in_specs=[
            pl.BlockSpec((1, tm, d), lambda b, t: (b, t, 0)),
            pl.BlockSpec((1, 1, 3 * d), lambda b, t: (b, 0, 0)),
            _const_spec(w_main.shape), _const_spec(w_f.shape),
        ],
        out_specs=[out_spec] * 6 + [pl.BlockSpec((1, H_C, tm), lambda b, t: (b, 0, t))],
        out_shape=[out] * 6 + [jax.ShapeDtypeStruct((bsz, H_C, s_len), F32)],
        compiler_params=_cparams("parallel", "parallel"),
        name="proj_cd",
    )(x, mod, w_main, w_f)


def _fox_scan_kernel(fl_ref, bf_ref, o_ref):
    z = fl_ref[0] + bf_ref[...]
    x = jnp.minimum(z, 0.0) - jnp.log1p(jnp.exp(-jnp.abs(z)))
    s_len = x.shape[-1]
    pos = lax.broadcasted_iota(jnp.int32, x.shape, 1)
    sh = 1
    while sh < s_len:
        x = x + jnp.where(pos >= sh, pltpu.roll(x, sh, 1), 0.0)
        sh *= 2
    o_ref[0] = -x


def _fox_scan(fl_t, b_f):
    bsz, h, s_len = fl_t.shape
    return pl.pallas_call(
        _fox_scan_kernel,
        grid=(bsz,),
        in_specs=[pl.BlockSpec((1, h, s_len), lambda b: (b, 0, 0)), _const_spec((h, 1))],
        out_specs=pl.BlockSpec((1, h, s_len), lambda b: (b, 0, 0)),
        out_shape=jax.ShapeDtypeStruct((bsz, h, s_len), F32),
        compiler_params=_cparams("parallel"),
        name="fox_scan",
    )(fl_t, b_f.reshape(h, 1))


def _t5_bias_kernel(tab_ref, o_ref):
    h = pl.program_id(0)
    t = o_ref.shape[-1]
    half = T5_BUCKETS // 2
    max_exact = half // 2
    qq = lax.broadcasted_iota(jnp.int32, (t, t), 0)
    kk = lax.broadcasted_iota(jnp.int32, (t, t), 1)
    far = tab_ref[h, half - 1]
    for which in range(2):
        rel = kk - qq - (t if which == 0 else 0)
        n = jnp.abs(rel)
        large = max_exact + (jnp.log(jnp.maximum(n, 1).astype(F32) / max_exact)
                             / math.log(T5_MAX_DIST / max_exact) * (half - max_exact)).astype(jnp.int32)
        large = jnp.minimum(large, half - 1)
        bucket = jnp.where(rel > 0, half, 0) + jnp.where(n < max_exact, n, large)
        val = jnp.zeros((t, t), F32)
        for b in range(T5_BUCKETS):
            val = jnp.where(bucket == b, tab_ref[h, b], val)
        o_ref[0, which] = val - far


def _t5_bias(t5_table, t):
    return pl.pallas_call(
        _t5_bias_kernel,
        grid=(H_B,),
        in_specs=[pl.BlockSpec(memory_space=pltpu.SMEM)],
        out_specs=pl.BlockSpec((1, 2, t, t), lambda h: (h, 0, 0, 0)),
        out_shape=jax.ShapeDtypeStruct((H_B, 2, t, t), F32),
        compiler_params=_cparams("parallel"),
        name="t5_bias",
    )(t5_table)


def _attn_kernel(*refs, mode, lam_init):
    if mode == "mla":
        q_ref, k_ref, v_ref, o_ref, m_ref, l_ref, acc_ref = refs
    elif mode == "fox":
        q_ref, k_ref, v_ref, nf_ref, o_ref, m_ref, l_ref, acc_ref = refs
    else:
        (q_ref, k_ref, v_ref, bias_ref, lq1_ref, lk1_ref, lq2_ref, lk2_ref, subg_ref,
         o_ref, m_ref, l_ref, acc_ref) = refs
    t = o_ref.shape[1]
    i = pl.program_id(2)
    lane = lax.broadcasted_iota(jnp.int32, (t, LANES), 1)
    low = lane < LANES // 2
    if mode == "mla":
        qs = (q_ref[0, 0], q_ref[0, 1])
    else:
        q = q_ref[0, 0]
        zero = jnp.zeros_like(q)
        qs = (jnp.where(low, q, zero), jnp.where(low, zero, q))

    m_ref[...] = jnp.full(m_ref.shape, NEG_INF, F32)
    l_ref[...] = jnp.zeros(l_ref.shape, F32)
    acc_ref[...] = jnp.zeros(acc_ref.shape, F32)

    def step(j, kind):
        v = v_ref[0, 0, j]
        for s in range(2):
            k = k_ref[0, s, j] if mode == "mla" else k_ref[0, 0, j]
            sc = _dot_nt(qs[s], k)
            if mode == "fox":
                sc = sc + nf_ref[0, s, j]
            if mode == "diff" and kind != "far":
                sc = sc + bias_ref[0, 0 if kind == "prev" else 1]
            if kind == "diag":
                row = lax.broadcasted_iota(jnp.int32, (t, t), 0)
                col = lax.broadcasted_iota(jnp.int32, (t, t), 1)
                if mode == "fox":
                    keep = col <= row
                else:
                    keep = (col // CHUNK) <= (row // CHUNK)
                sc = jnp.where(keep, sc, NEG_INF)
            m_prev = m_ref[s]
            m_new = jnp.maximum(m_prev, jnp.max(sc, axis=1, keepdims=True))
            alpha = jnp.exp(m_prev - m_new)
            p = jnp.exp(sc - m_new)
            l_ref[s] = alpha * l_ref[s] + jnp.sum(p, axis=1, keepdims=True)
            acc_ref[s] = alpha * acc_ref[s] + _dot(p.astype(BF16), v)
            m_ref[s] = m_new

    def far_body(j, carry):
        step(j, "far")
        return carry

    if mode == "diff":
        lax.fori_loop(0, jnp.maximum(i - 1, 0), far_body, 0)

        @pl.when(i > 0)
        def _():
            step(i - 1, "prev")
    else:
        lax.fori_loop(0, i, far_body, 0)
    step(i, "diag")

    o0 = acc_ref[0] / l_ref[0]
    o1 = acc_ref[1] / l_ref[1]
    if mode == "diff":
        lam = (jnp.exp(jnp.sum(lq1_ref[...] * lk1_ref[...])) - jnp.exp(jnp.sum(lq2_ref[...] * lk2_ref[...]))
               + lam_init)
        o = _rms_norm(o0 - lam * o1, subg_ref[...]) * (1.0 - lam_init)
    else:
        o = jnp.where(low, o0, o1)
    o_ref[0] = o.astype(o_ref.dtype)


def _attention(mode, q, k, v, extra=(), lam_init=0.0):
    bsz, _, s_len, _ = q.shape
    t = min(ATTN_TILE, s_len)
    nk = s_len // t
    n_grp = v.shape[1]
    per = 2 if mode == "mla" else 1
    k5 = k.reshape(bsz, k.shape[1], nk, t, LANES)
    v5 = v.reshape(bsz, n_grp, nk, t, LANES)
    in_specs = [
        pl.BlockSpec((1, per, t, LANES), lambda b, g, i: (b, g, i, 0)),
        pl.BlockSpec((1, per, nk, t, LANES), lambda b, g, i: (b, g, 0, 0, 0)),
        pl.BlockSpec((1, 1, nk, t, LANES), lambda b, g, i: (b, g, 0, 0, 0)),
    ]
    args = [q, k5, v5]
    if mode == "fox":
        (neg_f,) = extra
        args.append(neg_f.reshape(bsz, 2 * n_grp, nk, 1, t))
        in_specs.append(pl.BlockSpec((1, 2, nk, 1, t), lambda b, g, i: (b, g, 0, 0, 0)))
    elif mode == "diff":
        bias, lq1, lk1, lq2, lk2, sub_g = extra
        args += [bias, lq1.reshape(1, -1), lk1.reshape(1, -1), lq2.reshape(1, -1), lk2.reshape(1, -1),
                 sub_g.reshape(1, -1)]
        in_specs.append(pl.BlockSpec((1, 2, t, t), lambda b, g, i: (g, 0, 0, 0)))
        in_specs += [_const_spec((1, DH_B))] * 4 + [_const_spec((1, 2 * DH_B))]
    return pl.pallas_call(
        functools.partial(_attn_kernel, mode=mode, lam_init=lam_init),
        grid=(bsz, n_grp, s_len // t),
        in_specs=in_specs,
        out_specs=pl.BlockSpec((1, t, LANES), lambda b, g, i: (b, i, g)),
        out_shape=jax.ShapeDtypeStruct((bsz, s_len, n_grp * LANES), BF16),
        scratch_shapes=[pltpu.VMEM((2, t, 1), F32), pltpu.VMEM((2, t, 1), F32), pltpu.VMEM((2, t, LANES), F32)],
        compiler_params=_cparams("parallel", "parallel", "arbitrary"),
        name="attn_" + mode,
    )(*args)


def _band_bias_kernel(e_ref, o_ref):
    rows, win = o_ref.shape[1], o_ref.shape[2]
    ext = jnp.broadcast_to(e_ref[0], (rows, e_ref.shape[-1]))
    o_ref[0] = pltpu.roll(ext, win + 1, 1, stride=1, stride_axis=0)[:, :win]


def _band_bias(rel_table):
    ext_len = BAND_WIN + BAND_GROUP
    flipped = rel_table[:, ::-1]
    ext = jnp.pad(flipped, ((0, 0), (ext_len - flipped.shape[1], 0)), mode="edge")
    return pl.pallas_call(
        _band_bias_kernel,
        grid=(H_D,),
        in_specs=[pl.BlockSpec((1, 1, ext_len), lambda h: (h, 0, 0))],
        out_specs=pl.BlockSpec((1, BAND_GROUP, BAND_WIN), lambda h: (h, 0, 0)),
        out_shape=jax.ShapeDtypeStruct((H_D, BAND_GROUP, BAND_WIN), F32),
        compiler_params=_cparams("parallel"),
        name="band_bias",
    )(ext.reshape(H_D, 1, ext_len))


def _band_kernel(q_ref, kp_ref, kc_ref, vp_ref, vc_ref, bias_ref, o_ref):
    t = o_ref.shape[1]
    i = pl.program_id(2)
    lane = lax.broadcasted_iota(jnp.int32, (BAND_GROUP, LANES), 1)
    low = lane < LANES // 2
    kw = jnp.concatenate([kp_ref[0, 0], kc_ref[0, 0]], axis=0)
    vw = jnp.concatenate([vp_ref[0, 0], vc_ref[0, 0]], axis=0)
    row = lax.broadcasted_iota(jnp.int32, (BAND_GROUP, BAND_WIN), 0)
    col = lax.broadcasted_iota(jnp.int32, (BAND_GROUP, BAND_WIN), 1)
    first = (row // CHUNK) * CHUNK
    in_band = (col >= first) & (col < first + BAND_LEAD + CHUNK)
    for r in range(t // BAND_GROUP):
        q = q_ref[0, 0, r * BAND_GROUP:(r + 1) * BAND_GROUP]
        zero = jnp.zeros_like(q)
        w0 = t - BAND_LEAD + r * BAND_GROUP
        k = kw[w0:w0 + BAND_WIN]
        v = vw[w0:w0 + BAND_WIN]
        key_pos = col + (i * t - BAND_LEAD + r * BAND_GROUP)
        keep = in_band & (key_pos >= 0)
        outs = []
        for s in range(2):
            qs = jnp.where(low, q, zero) if s == 0 else jnp.where(low, zero, q)
            sc = _dot_nt(qs, k) + bias_ref[0, s]
            sc = jnp.where(keep, sc, NEG_INF)
            p = jnp.exp(sc - jnp.max(sc, axis=1, keepdims=True))
            denom = jnp.sum(p, axis=1, keepdims=True)
            outs.append(_dot(p.astype(BF16), v) / denom)
        o_ref[0, r * BAND_GROUP:(r + 1) * BAND_GROUP] = jnp.where(low, outs[0], outs[1]).astype(o_ref.dtype)


def _band_attention(q, k, v, bias):
    bsz, n_grp, s_len, _ = q.shape
    t = min(ATTN_TILE, s_len)
    assert t >= BAND_LEAD
    cur = lambda b, g, i: (b, g, i, 0)
    prev = lambda b, g, i: (b, g, jnp.maximum(i - 1, 0), 0)
    blk = (1, 1, t, LANES)
    return pl.pallas_call(
        _band_kernel,
        grid=(bsz, n_grp, s_len // t),
        in_specs=[pl.BlockSpec(blk, cur), pl.BlockSpec(blk, prev), pl.BlockSpec(blk, cur),
                  pl.BlockSpec(blk, prev), pl.BlockSpec(blk, cur),
                  pl.BlockSpec((1, 2, BAND_GROUP, BAND_WIN), lambda b, g, i: (g, 0, 0, 0))],
        out_specs=pl.BlockSpec((1, t, LANES), lambda b, g, i: (b, i, g)),
        out_shape=jax.ShapeDtypeStruct((bsz, s_len, n_grp * LANES), BF16),
        compiler_params=_cparams("parallel", "parallel", "parallel"),
        name="attn_band",
    )(q, k, k, v, v, bias.reshape(n_grp, 2, BAND_GROUP, BAND_WIN))


def _out_ln_kernel(oa_ref, ob_ref, w_ref, x_ref, mod_ref, g_ref, b_ref, o_ref):
    d = x_ref.shape[-1]
    half = oa_ref.shape[-1]
    y = _dot(oa_ref[0], w_ref[:half]) + _dot(ob_ref[0], w_ref[half:])
    gate = mod_ref[0][:, 2 * d:]
    o_ref[0] = _layer_norm(DEEPNORM_ALPHA * x_ref[0] + gate * y, g_ref[...], b_ref[...])


def _out_ln(oa, ob, w_out, x, mod, ln_g, ln_b):
    bsz, s_len, d = x.shape
    tm = min(ROW_TILE, s_len)
    row = lambda b, t: (b, t, 0)
    return pl.pallas_call(
        _out_ln_kernel,
        grid=(bsz, s_len // tm),
        in_specs=[
            pl.BlockSpec((1, tm, oa.shape[-1]), row), pl.BlockSpec((1, tm, ob.shape[-1]), row),
            _const_spec(w_out.shape),
            pl.BlockSpec((1, tm, d), row),
            pl.BlockSpec((1, 1, 3 * d), lambda b, t: (b, 0, 0)),
            _const_spec((1, d)), _const_spec((1, d)),
        ],
        out_specs=pl.BlockSpec((1, tm, d), row),
        out_shape=jax.ShapeDtypeStruct((bsz, s_len, d), F32),
        compiler_params=_cparams("parallel", "parallel"),
        name="out_ln",
    )(oa, ob, w_out.astype(BF16), x, mod, ln_g.reshape(1, d), ln_b.reshape(1, d))


def _ffn_kernel(x_ref, mod_ref, wg_ref, wv_ref, cw_ref, cb_ref, wd_ref, g_ref, b_ref, o_ref, carry_ref):
    d = x_ref.shape[-1]
    tm = x_ref.shape[1]
    x = x_ref[0]
    u, gate = _modulate(x, mod_ref[0], d)
    u = u.astype(BF16)
    g = _dot(u, wg_ref[...])
    val = _dot(u, wv_ref[...])

    @pl.when(pl.program_id(1) == 0)
    def _():
        carry_ref[...] = jnp.zeros(carry_ref.shape, F32)

    prev = carry_ref[...]
    row = lax.broadcasted_iota(jnp.int32, g.shape, 0)
    g1 = jnp.where(row == 0, prev[7:8], pltpu.roll(g, 1, 0))
    g2 = jnp.where(row == 0, prev[6:7], jnp.where(row == 1, prev[7:8], pltpu.roll(g, 2, 0)))
    carry_ref[...] = g[tm - 8:]
    cw = cw_ref[...]
    gc = cw[0:1] * g2 + cw[1:2] * g1 + cw[2:3] * g + cb_ref[...]
    hmid = (gc * jax.nn.sigmoid(gc) * val).astype(BF16)
    y = _dot(hmid, wd_ref[...])
    o_ref[0] = _layer_norm(DEEPNORM_ALPHA * x + gate * y, g_ref[...], b_ref[...])


def _ffn(x, mod, w_gate, w_val, conv_w, conv_b, w_down, ln_g, ln_b):
    bsz, s_len, d = x.shape
    d_ff = w_gate.shape[1]
    tm = min(FFN_ROW_TILE, s_len)
    row = lambda b, t: (b, t, 0)
    once = lambda shape: pl.BlockSpec(shape, lambda b, t: (0,) * len(shape), pipeline_mode=pl.Buffered(1))
    return pl.pallas_call(
        _ffn_kernel,
        grid=(bsz, s_len // tm),
        in_specs=[
            pl.BlockSpec((1, tm, d), row),
            pl.BlockSpec((1, 1, 3 * d), lambda b, t: (b, 0, 0)),
            once((d, d_ff)), once((d, d_ff)),
            _const_spec((CONV_W, d_ff)), _const_spec((1, d_ff)),
            once((d_ff, d)),
            _const_spec((1, d)), _const_spec((1, d)),
        ],
        out_specs=pl.BlockSpec((1, tm, d), row),
        out_shape=jax.ShapeDtypeStruct((bsz, s_len, d), F32),
        scratch_shapes=[pltpu.VMEM((8, d_ff), F32)],
        compiler_params=_cparams("parallel", "arbitrary"),
        name="ffn",
    )(x, mod, w_gate.astype(BF16), w_val.astype(BF16), conv_w, conv_b.reshape(1, d_ff),
      w_down.astype(BF16), ln_g.reshape(1, d), ln_b.reshape(1, d))


def kernel(x, c, ada_w, ada_b, ln_g, ln_b, t5_table, ab_w_in, mla_q_norm, mla_w_uq, mla_kv_norm, mla_w_ukv, diff_lq1, diff_lk1, diff_lq2, diff_lk2, diff_sub_g, ab_w_out, cd_w_in, fox_b_f, chunk_rel_table, cd_w_out, ffn_w_gate, ffn_w_val, ffn_conv_w, ffn_conv_b, ffn_w_down):
    bsz, s_len, d = x.shape
    depth = ada_w.shape[0]
    mods = _adaln(c, ada_w, ada_b).reshape(depth, 2, bsz, 1, 3 * d)
    t = min(ATTN_TILE, s_len)
    for i in range(depth):
        mod = mods[i, 0]
        if i % 2 == 0:
            e = i // 2
            qa, ka, va, qb, kb, vb = _proj_ab(x, mod, ab_w_in[e], mla_q_norm[e], mla_w_uq[e],
                                              mla_kv_norm[e], mla_w_ukv[e])
            o_first = _attention("mla", qa, ka, va)
            lam_init = 0.8 - 0.6 * math.exp(-0.3 * i)
            o_second = _attention("diff", qb, kb, vb,
                                  extra=(_t5_bias(t5_table, t), diff_lq1[e], diff_lk1[e], diff_lq2[e],
                                         diff_lk2[e], diff_sub_g[e]),
                                  lam_init=lam_init)
            w_out = ab_w_out[e]
        else:
            o = i // 2
            qc, kc, vc, qd, kd, vd, fl_t = _proj_cd(x, mod, cd_w_in[o])
            neg_f = _fox_scan(fl_t, fox_b_f[o])
            o_first = _attention("fox", qc, kc, vc, extra=(neg_f,))
            o_second = _band_attention(qd, kd, vd, _band_bias(chunk_rel_table[o]))
            w_out = cd_w_out[o]
        x = _out_ln(o_first, o_second, w_out, x, mod, ln_g[i, 0], ln_b[i, 0])
        x = _ffn(x, mods[i, 1], ffn_w_gate[i], ffn_w_val[i], ffn_conv_w[i], ffn_conv_b[i], ffn_w_down[i],
                 ln_g[i, 1], ln_b[i, 1])
    return x
```

```python
import functools
import math

import jax
import jax.numpy as jnp
from jax import lax
from jax.experimental import pallas as pl
from jax.experimental.pallas import tpu as pltpu

DEPTH = 2
CHUNK = 64
H_A, Q_LORA, KV_LORA, NOPE_DIM, ROPE_DIM, V_DIM_A = 8, 256, 128, 64, 32, 64
ROPE_BASE = 10000.0
H_B, DH_B, T5_BUCKETS, T5_MAX_DIST = 4, 64, 32, 128
H_C, DH_C = 8, 64
H_D, DH_D, BAND_CHUNKS, REL_CLIP = 8, 64, 8, 128
CONV_W = 3
EPS_LN = 1e-5
EPS_RMS = 1e-6
DEEPNORM_ALPHA = (2 * DEPTH) ** 0.25
NEG_INF = -1e30

LANES = 128
ATTN_TILE = 1024
BAND_TILE = 512
ROW_TILE = 512
FFN_ROW_TILE = 256
BAND_GROUP = 2 * CHUNK
BAND_LEAD = BAND_CHUNKS * CHUNK
BAND_WIN = BAND_LEAD + BAND_GROUP
VMEM_LIMIT = 56 * 1024 * 1024

BF16 = jnp.bfloat16
F32 = jnp.float32


def _cparams(*sem, flags=None):
    return pltpu.CompilerParams(dimension_semantics=sem, vmem_limit_bytes=VMEM_LIMIT, flags=flags)


def _dot(a, b):
    return jnp.dot(a, b, preferred_element_type=F32)


def _dot_nt(a, b):
    return lax.dot_general(a, b, (((1,), (1,)), ((), ())), preferred_element_type=F32)


def _layer_norm(z, g, b):
    mu = jnp.mean(z, axis=-1, keepdims=True)
    zc = z - mu
    var = jnp.mean(zc * zc, axis=-1, keepdims=True)
    return zc * lax.rsqrt(var + EPS_LN) * g + b


def _rms_norm(z, g):
    ms = jnp.mean(z * z, axis=-1, keepdims=True)
    return z * lax.rsqrt(ms + EPS_RMS) * g


def _const_spec(shape):
    return pl.BlockSpec(shape, lambda *_: (0,) * len(shape))


def _adaln_kernel(c_ref, w_ref, b_ref, o_ref):
    c = c_ref[...]
    cond = c * jax.nn.sigmoid(c)
    y = jnp.dot(cond, w_ref[0], preferred_element_type=F32, precision=lax.Precision.HIGHEST)
    one = jnp.where(pl.program_id(1) > 0, 1.0, 0.0).astype(F32)
    o_ref[0] = y + b_ref[0] + one


def _adaln(c, ada_w, ada_b):
    bsz, d = c.shape
    n_sub = ada_w.shape[0] * ada_w.shape[1]
    w = ada_w.reshape(n_sub, d, 3 * d)
    b = ada_b.reshape(n_sub, 1, 3 * d)
    return pl.pallas_call(
        _adaln_kernel,
        grid=(n_sub, 3),
        in_specs=[
            pl.BlockSpec((bsz, d), lambda l, n: (0, 0)),
            pl.BlockSpec((1, d, d), lambda l, n: (l, 0, n)),
            pl.BlockSpec((1, 1, d), lambda l, n: (l, 0, n)),
        ],
        out_specs=pl.BlockSpec((1, bsz, d), lambda l, n: (l, 0, n)),
        out_shape=jax.ShapeDtypeStruct((n_sub, bsz, 3 * d), F32),
        compiler_params=_cparams("arbitrary", "arbitrary"),
        name="adaln",
    )(c, w, b)


def _modulate(x, mod, d):
    return x * mod[:, d:2 * d] + mod[:, :d], mod[:, 2 * d:]


def _proj_ab_kernel(x_ref, mod_ref, w_in_ref, qn_ref, kvn_ref, wq_ref, wk_ref, wv_ref,
                    cq_ref, sq_ref, ck_ref, sk_ref,
                    qa_ref, ka_ref, va_ref, qb_ref, kb_ref, vb_ref):
    d = x_ref.shape[-1]
    u, _ = _modulate(x_ref[0], mod_ref[0], d)
    h = _dot(u.astype(BF16), w_in_ref[...])
    o = 0
    cq = h[:, o:o + Q_LORA]; o += Q_LORA
    ckv = h[:, o:o + KV_LORA]; o += KV_LORA
    kr_a = h[:, o:o + LANES]; o += LANES
    kr_b = h[:, o:o + LANES]; o += LANES
    wb = H_B * LANES
    qb = h[:, o:o + wb]; o += wb
    kb = h[:, o:o + wb]; o += wb
    vb = h[:, o:o + wb]

    nq = _rms_norm(cq, qn_ref[...]).astype(BF16)
    q12 = _dot(nq, wq_ref[...])
    cq_t, sq_t = cq_ref[...], sq_ref[...]
    for hh in range(H_A):
        a = q12[:, hh * LANES:(hh + 1) * LANES]
        b = q12[:, (H_A + hh) * LANES:(H_A + hh + 1) * LANES]
        qa_ref[0, hh] = (a * cq_t + b * sq_t).astype(BF16)

    nkv = _rms_norm(ckv, kvn_ref[...]).astype(BF16)
    kn = _dot(nkv, wk_ref[...])
    vv = _dot(nkv, wv_ref[...])
    k_rope = kr_a * ck_ref[...] + kr_b * sk_ref[...]
    for hh in range(H_A):
        ka_ref[0, hh] = (kn[:, hh * LANES:(hh + 1) * LANES] + k_rope).astype(BF16)
    for p in range(H_A // 2):
        va_ref[0, p] = vv[:, p * LANES:(p + 1) * LANES].astype(BF16)
    scale_b = DH_B ** -0.5
    for hh in range(H_B):
        sl = slice(hh * LANES, (hh + 1) * LANES)
        qb_ref[0, hh] = (qb[:, sl] * scale_b).astype(BF16)
        kb_ref[0, hh] = kb[:, sl].astype(BF16)
        vb_ref[0, hh] = vb[:, sl].astype(BF16)


def _rope_tables(s_len):
    half = ROPE_DIM // 2
    inv = jnp.power(ROPE_BASE, -jnp.arange(half, dtype=F32) / half)
    ang = jnp.arange(s_len, dtype=F32)[:, None] * inv[None, :]
    cos, sin = jnp.cos(ang), jnp.sin(ang)
    zeros_pad = jnp.zeros((s_len, LANES - NOPE_DIM - ROPE_DIM), F32)
    c_rope = jnp.concatenate([cos, cos], axis=1)
    s_rope = jnp.concatenate([-sin, sin], axis=1)
    cq = jnp.concatenate([jnp.ones((s_len, NOPE_DIM), F32), c_rope, zeros_pad], axis=1)
    ck = jnp.concatenate([jnp.zeros((s_len, NOPE_DIM), F32), c_rope, zeros_pad], axis=1)
    sk = jnp.concatenate([jnp.zeros((s_len, NOPE_DIM), F32), s_rope, zeros_pad], axis=1)
    scale_a = (NOPE_DIM + ROPE_DIM) ** -0.5
    return cq * scale_a, sk * scale_a, ck, sk


def _swap_halves(w):
    half = w.shape[-1] // 2
    return jnp.concatenate([w[..., half:], w[..., :half]], axis=-1)


def _proj_ab_weights(w_in, w_uq, w_ukv):
    d = w_in.shape[0]
    wb = H_B * 2 * DH_B
    o = 0
    w_cq = w_in[:, o:o + Q_LORA]; o += Q_LORA
    w_ckv = w_in[:, o:o + KV_LORA]; o += KV_LORA
    w_kr = w_in[:, o:o + ROPE_DIM]; o += ROPE_DIM
    w_rest = w_in[:, o:o + 3 * wb]
    lead = jnp.zeros((d, NOPE_DIM), F32)
    tail = jnp.zeros((d, LANES - NOPE_DIM - ROPE_DIM), F32)
    w_kr_a = jnp.concatenate([lead, w_kr, tail], axis=1)
    w_kr_b = jnp.concatenate([lead, _swap_halves(w_kr), tail], axis=1)
    w_in_aug = jnp.concatenate([w_cq, w_ckv, w_kr_a, w_kr_b, w_rest], axis=1).astype(BF16)

    wq = w_uq.reshape(Q_LORA, H_A, NOPE_DIM + ROPE_DIM)
    wq_nope, wq_rope = wq[..., :NOPE_DIM], wq[..., NOPE_DIM:]
    zpad = jnp.zeros((Q_LORA, H_A, LANES - NOPE_DIM - ROPE_DIM), F32)
    wq1 = jnp.concatenate([wq_nope, wq_rope, zpad], axis=-1).reshape(Q_LORA, H_A * LANES)
    wq2 = jnp.concatenate([jnp.zeros_like(wq_nope), _swap_halves(wq_rope), zpad], axis=-1)
    wq12 = jnp.concatenate([wq1, wq2.reshape(Q_LORA, H_A * LANES)], axis=1).astype(BF16)

    wkv = w_ukv.reshape(KV_LORA, H_A, NOPE_DIM + V_DIM_A)
    wk = jnp.concatenate([wkv[..., :NOPE_DIM], jnp.zeros((KV_LORA, H_A, LANES - NOPE_DIM), F32)], axis=-1)
    wk = wk.reshape(KV_LORA, H_A * LANES).astype(BF16)
    wv = wkv[..., NOPE_DIM:].reshape(KV_LORA, H_A * V_DIM_A).astype(BF16)
    return w_in_aug, wq12, wk, wv


def _proj_ab(x, mod, w_in, q_norm, w_uq, kv_norm, w_ukv):
    bsz, s_len, d = x.shape
    tm = min(ROW_TILE, s_len)
    w_in_aug, wq12, wk, wv = _proj_ab_weights(w_in, w_uq, w_ukv)
    tabs = _rope_tables(s_len)
    row = lambda b, t: (b, t, 0)
    head = lambda b, t: (b, 0, t, 0)
    tab_spec = pl.BlockSpec((tm, LANES), lambda b, t: (t, 0))
    out = lambda h: jax.ShapeDtypeStruct((bsz, h, s_len, LANES), BF16)
    out_spec = lambda h: pl.BlockSpec((1, h, tm, LANES), head)
    return pl.pallas_call(
        _proj_ab_kernel,
        grid=(bsz, s_len // tm),
        in_specs=[
            pl.BlockSpec((1, tm, d), row),
            pl.BlockSpec((1, 1, 3 * d), lambda b, t: (b, 0, 0)),
            _const_spec(w_in_aug.shape),
            _const_spec((1, Q_LORA)), _const_spec((1, KV_LORA)),
            _const_spec(wq12.shape), _const_spec(wk.shape), _const_spec(wv.shape),
            tab_spec, tab_spec, tab_spec, tab_spec,
        ],
        out_specs=[out_spec(H_A), out_spec(H_A), out_spec(H_A // 2),
                   out_spec(H_B), out_spec(H_B), out_spec(H_B)],
        out_shape=[out(H_A), out(H_A), out(H_A // 2), out(H_B), out(H_B), out(H_B)],
        compiler_params=_cparams("parallel", "parallel"),
        name="proj_ab",
    )(x, mod, w_in_aug, q_norm.reshape(1, -1), kv_norm.reshape(1, -1), wq12, wk, wv, *tabs)


def _proj_cd_kernel(x_ref, mod_ref, w_ref, wf_ref, qc_ref, kc_ref, vc_ref, qd_ref, kd_ref, vd_ref, fl_ref):
    d = x_ref.shape[-1]
    u, _ = _modulate(x_ref[0], mod_ref[0], d)
    u = u.astype(BF16)
    h = _dot(u, w_ref[...])
    fl_ref[0] = _dot_nt(wf_ref[...], u)
    scale = DH_C ** -0.5
    n_pair = H_C // 2
    for k, (ref, sc) in enumerate(((qc_ref, scale), (kc_ref, None), (vc_ref, None),
                                   (qd_ref, scale), (kd_ref, None), (vd_ref, None))):
        for p in range(n_pair):
            blk = h[:, (k * n_pair + p) * LANES:(k * n_pair + p + 1) * LANES]
            if sc is not None:
                blk = blk * sc
            ref[0, p] = blk.astype(BF16)


def _proj_cd(x, mod, w_in):
    bsz, s_len, d = x.shape
    tm = min(ROW_TILE, s_len)
    wc, wd = H_C * DH_C, H_D * DH_D
    o = 3 * wc
    w_main = jnp.concatenate([w_in[:, :o], w_in[:, o + H_C:]], axis=1).astype(BF16)
    w_f = w_in[:, o:o + H_C].T.astype(BF16)
    n_pair = H_C // 2
    head = lambda b, t: (b, 0, t, 0)
    out = jax.ShapeDtypeStruct((bsz, n_pair, s_len, LANES), BF16)
    out_spec = pl.BlockSpec((1, n_pair, tm, LANES), head)
    return pl.pallas_call(
        _proj_cd_kernel,
        grid=(bsz, s_len // tm),
        in_specs=[
            pl.BlockSpec((1, tm, d), lambda b, t: (b, t, 0)),
            pl.BlockSpec((1, 1, 3 * d), lambda b, t: (b, 0, 0)),
            _const_spec(w_main.shape), _const_spec(w_f.shape),
        ],
        out_specs=[out_spec] * 6 + [pl.BlockSpec((1, H_C, tm), lambda b, t: (b, 0, t))],
        out_shape=[out] * 6 + [jax.ShapeDtypeStruct((bsz, H_C, s_len), F32)],
        compiler_params=_cparams("parallel", "parallel"),
        name="proj_cd",
    )(x, mod, w_main, w_f)


def _fox_scan_kernel(fl_ref, bf_ref, o_ref):
    z = fl_ref[0] + bf_ref[...]
    x = jnp.minimum(z, 0.0) - jnp.log1p(jnp.exp(-jnp.abs(z)))
    s_len = x.shape[-1]
    pos = lax.broadcasted_iota(jnp.int32, x.shape, 1)
    sh = 1
    while sh < s_len:
        x = x + jnp.where(pos >= sh, pltpu.roll(x, sh, 1), 0.0)
        sh *= 2
    o_ref[0] = -x


def _fox_scan(fl_t, b_f):
    bsz, h, s_len = fl_t.shape
    return pl.pallas_call(
        _fox_scan_kernel,
        grid=(bsz,),
        in_specs=[pl.BlockSpec((1, h, s_len), lambda b: (b, 0, 0)), _const_spec((h, 1))],
        out_specs=pl.BlockSpec((1, h, s_len), lambda b: (b, 0, 0)),
        out_shape=jax.ShapeDtypeStruct((bsz, h, s_len), F32),
        compiler_params=_cparams("parallel"),
        name="fox_scan",
    )(fl_t, b_f.reshape(h, 1))


def _t5_bias_kernel(tab_ref, o_ref):
    h = pl.program_id(0)
    half = T5_BUCKETS // 2
    max_exact = half // 2
    shape = (LANES, 2 * LANES)
    qq = lax.broadcasted_iota(jnp.int32, shape, 0)
    kk = lax.broadcasted_iota(jnp.int32, shape, 1)
    rel = kk - LANES - qq
    n = jnp.abs(rel)
    large = max_exact + (jnp.log(jnp.maximum(n, 1).astype(F32) / max_exact)
                         / math.log(T5_MAX_DIST / max_exact) * (half - max_exact)).astype(jnp.int32)
    large = jnp.minimum(large, half - 1)
    bucket = jnp.where(rel > 0, half, 0) + jnp.where(n < max_exact, n, large)
    val = jnp.zeros(shape, F32)
    for b in range(T5_BUCKETS):
        val = jnp.where(bucket == b, tab_ref[h, b], val)
    o_ref[0] = val - tab_ref[h, half - 1]


def _t5_bias(t5_table):
    return pl.pallas_call(
        _t5_bias_kernel,
        grid=(H_B,),
        in_specs=[pl.BlockSpec(memory_space=pltpu.SMEM)],
        out_specs=pl.BlockSpec((1, LANES, 2 * LANES), lambda h: (h, 0, 0)),
        out_shape=jax.ShapeDtypeStruct((H_B, LANES, 2 * LANES), F32),
        compiler_params=_cparams("parallel"),
        name="t5_bias",
    )(t5_table)


def _attn_kernel(*refs, mode, lam_init):
    if mode == "mla":
        q_ref, k_ref, v_ref, o_ref, m_ref, l_ref, acc_ref = refs
    elif mode == "fox":
        q_ref, k_ref, v_ref, nf_ref, o_ref, m_ref, l_ref, acc_ref = refs
    else:
        (q_ref, k_ref, v_ref, bias_ref, lq1_ref, lk1_ref, lq2_ref, lk2_ref, subg_ref,
         o_ref, m_ref, l_ref, acc_ref, s_ref) = refs
    t = o_ref.shape[1]
    i = pl.program_id(2)
    lane = lax.broadcasted_iota(jnp.int32, (t, LANES), 1)
    low = lane < LANES // 2
    if mode == "mla":
        qs = (q_ref[0, 0], q_ref[0, 1])
    else:
        q = q_ref[0, 0]
        zero = jnp.zeros_like(q)
        qs = (jnp.where(low, q, zero), jnp.where(low, zero, q))

    m_ref[...] = jnp.full(m_ref.shape, NEG_INF, F32)
    l_ref[...] = jnp.zeros(l_ref.shape, F32)
    acc_ref[...] = jnp.zeros(acc_ref.shape, F32)

    def step(j, kind):
        v = v_ref[0, 0, j]
        for s in range(2):
            k = k_ref[0, s, j] if mode == "mla" else k_ref[0, 0, j]
            sc = _dot_nt(qs[s], k)
            if mode == "fox":
                sc = sc + nf_ref[0, s, j]
            if mode == "diff" and kind != "far":
                s_ref[...] = sc
                if kind == "prev":
                    s_ref[:LANES, t - LANES:] += bias_ref[0, :, :LANES]
                else:
                    s_ref[:LANES, :LANES] += bias_ref[0, :, LANES:]
                    for r in range(1, t // LANES):
                        s_ref[r * LANES:(r + 1) * LANES, (r - 1) * LANES:(r + 1) * LANES] += bias_ref[0]
                sc = s_ref[...]
            if kind == "diag":
                row = lax.broadcasted_iota(jnp.int32, (t, t), 0)
                col = lax.broadcasted_iota(jnp.int32, (t, t), 1)
                if mode == "fox":
                    keep = col <= row
                else:
                    keep = (col // CHUNK) <= (row // CHUNK)
                sc = jnp.where(keep, sc, NEG_INF)
            m_prev = m_ref[s]
            m_new = jnp.maximum(m_prev, jnp.max(sc, axis=1, keepdims=True))
            alpha = jnp.exp(m_prev - m_new)
            p = jnp.exp(sc - m_new)
            l_ref[s] = alpha * l_ref[s] + jnp.sum(p, axis=1, keepdims=True)
            acc_ref[s] = alpha * acc_ref[s] + _dot(p.astype(BF16), v)
            m_ref[s] = m_new

    def far_body(j, carry):
        step(j, "far")
        return carry

    if mode == "diff":
        lax.fori_loop(0, jnp.maximum(i - 1, 0), far_body, 0)

        @pl.when(i > 0)
        def _():
            step(i - 1, "prev")
    else:
        lax.fori_loop(0, i, far_body, 0)
    step(i, "diag")

    o0 = acc_ref[0] / l_ref[0]
    o1 = acc_ref[1] / l_ref[1]
    if mode == "diff":
        lam = (jnp.exp(jnp.sum(lq1_ref[...] * lk1_ref[...])) - jnp.exp(jnp.sum(lq2_ref[...] * lk2_ref[...]))
               + lam_init)
        o = _rms_norm(o0 - lam * o1, subg_ref[...]) * (1.0 - lam_init)
    else:
        o = jnp.where(low, o0, o1)
    o_ref[0] = o.astype(o_ref.dtype)


def _attention(mode, q, k, v, extra=(), lam_init=0.0):
    bsz, _, s_len, _ = q.shape
    t = min(ATTN_TILE, s_len)
    nk = s_len // t
    n_grp = v.shape[1]
    per = 2 if mode == "mla" else 1
    k5 = k.reshape(bsz, k.shape[1], nk, t, LANES)
    v5 = v.reshape(bsz, n_grp, nk, t, LANES)
    in_specs = [
        pl.BlockSpec((1, per, t, LANES), lambda b, g, i: (b, g, i, 0)),
        pl.BlockSpec((1, per, nk, t, LANES), lambda b, g, i: (b, g, 0, 0, 0)),
        pl.BlockSpec((1, 1, nk, t, LANES), lambda b, g, i: (b, g, 0, 0, 0)),
    ]
    args = [q, k5, v5]
    if mode == "fox":
        (neg_f,) = extra
        args.append(neg_f.reshape(bsz, 2 * n_grp, nk, 1, t))
        in_specs.append(pl.BlockSpec((1, 2, nk, 1, t), lambda b, g, i: (b, g, 0, 0, 0)))
    elif mode == "diff":
        bias, lq1, lk1, lq2, lk2, sub_g = extra
        args += [bias, lq1.reshape(1, -1), lk1.reshape(1, -1), lq2.reshape(1, -1), lk2.reshape(1, -1),
                 sub_g.reshape(1, -1)]
        in_specs.append(pl.BlockSpec((1, LANES, 2 * LANES), lambda b, g, i: (g, 0, 0)))
        in_specs += [_const_spec((1, DH_B))] * 4 + [_const_spec((1, 2 * DH_B))]
    scratch = [pltpu.VMEM((2, t, 1), F32), pltpu.VMEM((2, t, 1), F32), pltpu.VMEM((2, t, LANES), F32)]
    if mode == "diff":
        scratch.append(pltpu.VMEM((t, t), F32))
    return pl.pallas_call(
        functools.partial(_attn_kernel, mode=mode, lam_init=lam_init),
        grid=(bsz, n_grp, s_len // t),
        in_specs=in_specs,
        out_specs=pl.BlockSpec((1, t, LANES), lambda b, g, i: (b, i, g)),
        out_shape=jax.ShapeDtypeStruct((bsz, s_len, n_grp * LANES), BF16),
        scratch_shapes=scratch,
        compiler_params=_cparams("parallel", "parallel", "arbitrary"),
        name="attn_" + mode,
    )(*args)


def _band_bias_kernel(e_ref, o_ref):
    rows, win = o_ref.shape[1], o_ref.shape[2]
    ext = jnp.broadcast_to(e_ref[0], (rows, e_ref.shape[-1]))
    o_ref[0] = pltpu.roll(ext, win + 1, 1, stride=1, stride_axis=0)[:, :win]


def _band_bias(rel_table):
    ext_len = BAND_WIN + BAND_GROUP
    flipped = rel_table[:, ::-1]
    ext = jnp.pad(flipped, ((0, 0), (ext_len - flipped.shape[1], 0)), mode="edge")
    return pl.pallas_call(
        _band_bias_kernel,
        grid=(H_D,),
        in_specs=[pl.BlockSpec((1, 1, ext_len), lambda h: (h, 0, 0))],
        out_specs=pl.BlockSpec((1, BAND_GROUP, BAND_WIN), lambda h: (h, 0, 0)),
        out_shape=jax.ShapeDtypeStruct((H_D, BAND_GROUP, BAND_WIN), F32),
        compiler_params=_cparams("parallel"),
        name="band_bias",
    )(ext.reshape(H_D, 1, ext_len))


def _band_kernel(q_ref, kp_ref, kc_ref, vp_ref, vc_ref, bias_ref, o_ref):
    t = o_ref.shape[1]
    i = pl.program_id(2)
    lane = lax.broadcasted_iota(jnp.int32, (BAND_GROUP, LANES), 1)
    low = lane < LANES // 2
    kw = jnp.concatenate([kp_ref[0, 0], kc_ref[0, 0]], axis=0)
    vw = jnp.concatenate([vp_ref[0, 0], vc_ref[0, 0]], axis=0)
    row = lax.broadcasted_iota(jnp.int32, (BAND_GROUP, BAND_WIN), 0)
    col = lax.broadcasted_iota(jnp.int32, (BAND_GROUP, BAND_WIN), 1)
    first = (row // CHUNK) * CHUNK
    in_band = (col >= first) & (col < first + BAND_LEAD + CHUNK)
    for r in range(t // BAND_GROUP):
        q = q_ref[0, 0, r * BAND_GROUP:(r + 1) * BAND_GROUP]
        zero = jnp.zeros_like(q)
        w0 = t - BAND_LEAD + r * BAND_GROUP
        k = kw[w0:w0 + BAND_WIN]
        v = vw[w0:w0 + BAND_WIN]
        key_pos = col + (i * t - BAND_LEAD + r * BAND_GROUP)
        keep = in_band & (key_pos >= 0)
        outs = []
        for s in range(2):
            qs = jnp.where(low, q, zero) if s == 0 else jnp.where(low, zero, q)
            sc = _dot_nt(qs, k) + bias_ref[0, s]
            sc = jnp.where(keep, sc, NEG_INF)
            p = jnp.exp(sc - jnp.max(sc, axis=1, keepdims=True))
            denom = jnp.sum(p, axis=1, keepdims=True)
            outs.append(_dot(p.astype(BF16), v) / denom)
        o_ref[0, r * BAND_GROUP:(r + 1) * BAND_GROUP] = jnp.where(low, outs[0], outs[1]).astype(o_ref.dtype)


def _band_attention(q, k, v, bias):
    bsz, n_grp, s_len, _ = q.shape
    t = min(BAND_TILE, s_len)
    assert t >= BAND_LEAD
    cur = lambda b, g, i: (b, g, i, 0)
    prev = lambda b, g, i: (b, g, jnp.maximum(i - 1, 0), 0)
    blk = (1, 1, t, LANES)
    return pl.pallas_call(
        _band_kernel,
        grid=(bsz, n_grp, s_len // t),
        in_specs=[pl.BlockSpec(blk, cur), pl.BlockSpec(blk, prev), pl.BlockSpec(blk, cur),
                  pl.BlockSpec(blk, prev), pl.BlockSpec(blk, cur),
                  pl.BlockSpec((1, 2, BAND_GROUP, BAND_WIN), lambda b, g, i: (g, 0, 0, 0))],
        out_specs=pl.BlockSpec((1, t, LANES), lambda b, g, i: (b, i, g)),
        out_shape=jax.ShapeDtypeStruct((bsz, s_len, n_grp * LANES), BF16),
        compiler_params=_cparams("parallel", "parallel", "parallel"),
        name="attn_band",
    )(q, k, k, v, v, bias.reshape(n_grp, 2, BAND_GROUP, BAND_WIN))


def _out_ln_kernel(oa_ref, ob_ref, w_ref, x_ref, mod_ref, g_ref, b_ref, o_ref):
    d = x_ref.shape[-1]
    half = oa_ref.shape[-1]
    y = _dot(oa_ref[0], w_ref[:half]) + _dot(ob_ref[0], w_ref[half:])
    gate = mod_ref[0][:, 2 * d:]
    o_ref[0] = _layer_norm(DEEPNORM_ALPHA * x_ref[0] + gate * y, g_ref[...], b_ref[...])


def _out_ln(oa, ob, w_out, x, mod, ln_g, ln_b):
    bsz, s_len, d = x.shape
    tm = min(ROW_TILE, s_len)
    row = lambda b, t: (b, t, 0)
    return pl.pallas_call(
        _out_ln_kernel,
        grid=(bsz, s_len // tm),
        in_specs=[
            pl.BlockSpec((1, tm, oa.shape[-1]), row), pl.BlockSpec((1, tm, ob.shape[-1]), row),
            _const_spec(w_out.shape),
            pl.BlockSpec((1, tm, d), row),
            pl.BlockSpec((1, 1, 3 * d), lambda b, t: (b, 0, 0)),
            _const_spec((1, d)), _const_spec((1, d)),
        ],
        out_specs=pl.BlockSpec((1, tm, d), row),
        out_shape=jax.ShapeDtypeStruct((bsz, s_len, d), F32),
        compiler_params=_cparams("parallel", "parallel"),
        name="out_ln",
    )(oa, ob, w_out.astype(BF16), x, mod, ln_g.reshape(1, d), ln_b.reshape(1, d))


def _ffn_kernel(x_ref, mod_ref, wg_ref, wv_ref, cw_ref, cb_ref, wd_ref, g_ref, b_ref, o_ref, carry_ref):
    d = x_ref.shape[-1]
    tm = x_ref.shape[1]
    x = x_ref[0]
    u, gate = _modulate(x, mod_ref[0], d)
    u = u.astype(BF16)
    g = _dot(u, wg_ref[...])
    val = _dot(u, wv_ref[...])

    @pl.when(pl.program_id(1) == 0)
    def _():
        carry_ref[...] = jnp.zeros(carry_ref.shape, F32)

    prev = carry_ref[...]
    row = lax.broadcasted_iota(jnp.int32, g.shape, 0)
    g1 = jnp.where(row == 0, prev[7:8], pltpu.roll(g, 1, 0))
    g2 = jnp.where(row == 0, prev[6:7], jnp.where(row == 1, prev[7:8], pltpu.roll(g, 2, 0)))
    carry_ref[...] = g[tm - 8:]
    cw = cw_ref[...]
    gc = cw[0:1] * g2 + cw[1:2] * g1 + cw[2:3] * g + cb_ref[...]
    hmid = (gc * jax.nn.sigmoid(gc) * val).astype(BF16)
    y = _dot(hmid, wd_ref[...])
    o_ref[0] = _layer_norm(DEEPNORM_ALPHA * x + gate * y, g_ref[...], b_ref[...])


def _ffn(x, mod, w_gate, w_val, conv_w, conv_b, w_down, ln_g, ln_b):
    bsz, s_len, d = x.shape
    d_ff = w_gate.shape[1]
    tm = min(FFN_ROW_TILE, s_len)
    row = lambda b, t: (b, t, 0)
    once = lambda shape: pl.BlockSpec(shape, lambda b, t: (0,) * len(shape), pipeline_mode=pl.Buffered(1))
    return pl.pallas_call(
        _ffn_kernel,
        grid=(bsz, s_len // tm),
        in_specs=[
            pl.BlockSpec((1, tm, d), row),
            pl.BlockSpec((1, 1, 3 * d), lambda b, t: (b, 0, 0)),
            once((d, d_ff)), once((d, d_ff)),
            _const_spec((CONV_W, d_ff)), _const_spec((1, d_ff)),
            once((d_ff, d)),
            _const_spec((1, d)), _const_spec((1, d)),
        ],
        out_specs=pl.BlockSpec((1, tm, d), row),
        out_shape=jax.ShapeDtypeStruct((bsz, s_len, d), F32),
        scratch_shapes=[pltpu.VMEM((8, d_ff), F32)],
        compiler_params=_cparams("parallel", "arbitrary"),
        name="ffn",
    )(x, mod, w_gate.astype(BF16), w_val.astype(BF16), conv_w, conv_b.reshape(1, d_ff),
      w_down.astype(BF16), ln_g.reshape(1, d), ln_b.reshape(1, d))


def kernel(x, c, ada_w, ada_b, ln_g, ln_b, t5_table, ab_w_in, mla_q_norm, mla_w_uq, mla_kv_norm, mla_w_ukv, diff_lq1, diff_lk1, diff_lq2, diff_lk2, diff_sub_g, ab_w_out, cd_w_in, fox_b_f, chunk_rel_table, cd_w_out, ffn_w_gate, ffn_w_val, ffn_conv_w, ffn_conv_b, ffn_w_down):
    bsz, s_len, d = x.shape
    depth = ada_w.shape[0]
    mods = _adaln(c, ada_w, ada_b).reshape(depth, 2, bsz, 1, 3 * d)
    for i in range(depth):
        mod = mods[i, 0]
        if i % 2 == 0:
            e = i // 2
            qa, ka, va, qb, kb, vb = _proj_ab(x, mod, ab_w_in[e], mla_q_norm[e], mla_w_uq[e],
                                              mla_kv_norm[e], mla_w_ukv[e])
            o_first = _attention("mla", qa, ka, va)
            lam_init = 0.8 - 0.6 * math.exp(-0.3 * i)
            o_second = _attention("diff", qb, kb, vb,
                                  extra=(_t5_bias(t5_table), diff_lq1[e], diff_lk1[e], diff_lq2[e],
                                         diff_lk2[e], diff_sub_g[e]),
                                  lam_init=lam_init)
            w_out = ab_w_out[e]
        else:
            o = i // 2
            qc, kc, vc, qd, kd, vd, fl_t = _proj_cd(x, mod, cd_w_in[o])
            neg_f = _fox_scan(fl_t, fox_b_f[o])
            o_first = _attention("fox", qc, kc, vc, extra=(neg_f,))
            o_second = _band_attention(qd, kd, vd, _band_bias(chunk_rel_table[o]))
            w_out = cd_w_out[o]
        x = _out_ln(o_first, o_second, w_out, x, mod, ln_g[i, 0], ln_b[i, 0])
        x = _ffn(x, mods[i, 1], ffn_w_gate[i], ffn_w_val[i], ffn_conv_w[i], ffn_conv_b[i], ffn_w_down[i],
                 ln_g[i, 1], ln_b[i, 1])
    return x
```

```python
import functools
import math

import jax
import jax.numpy as jnp
from jax import lax
from jax.experimental import pallas as pl
from jax.experimental.pallas import tpu as pltpu

DEPTH = 2
CHUNK = 64
H_A, Q_LORA, KV_LORA, NOPE_DIM, ROPE_DIM, V_DIM_A = 8, 256, 128, 64, 32, 64
ROPE_BASE = 10000.0
H_B, DH_B, T5_BUCKETS, T5_MAX_DIST = 4, 64, 32, 128
H_C, DH_C = 8, 64
H_D, DH_D, BAND_CHUNKS, REL_CLIP = 8, 64, 8, 128
CONV_W = 3
EPS_LN = 1e-5
EPS_RMS = 1e-6
DEEPNORM_ALPHA = (2 * DEPTH) ** 0.25
NEG_INF = -1e30
LOG2E = math.log2(math.e)

MXU_COLS = 256
LANES = 128
ATTN_TILE = 1024
STRIP = 64
BAND_TILE = 512
ROW_TILE = 512
FFN_ROW_TILE = 512
FFN_COL_CHUNKS = 2
BAND_GROUP = 2 * CHUNK
BAND_LEAD = BAND_CHUNKS * CHUNK
BAND_WIN = BAND_LEAD + BAND_GROUP
VMEM_LIMIT = 56 * 1024 * 1024

BF16 = jnp.bfloat16
F32 = jnp.float32


def _cparams(*sem, flags=None):
    return pltpu.CompilerParams(dimension_semantics=sem, vmem_limit_bytes=VMEM_LIMIT, flags=flags)


def _dot(a, b):
    return jnp.dot(a, b, preferred_element_type=F32)


def _dot_nt(a, b):
    return lax.dot_general(a, b, (((1,), (1,)), ((), ())), preferred_element_type=F32)


def _layer_norm(z, g, b):
    mu = jnp.mean(z, axis=-1, keepdims=True)
    zc = z - mu
    var = jnp.mean(zc * zc, axis=-1, keepdims=True)
    return zc * lax.rsqrt(var + EPS_LN) * g + b


def _rms_norm(z, g):
    ms = jnp.mean(z * z, axis=-1, keepdims=True)
    return z * lax.rsqrt(ms + EPS_RMS) * g


def _const_spec(shape):
    return pl.BlockSpec(shape, lambda *_: (0,) * len(shape))


def _adaln_kernel(c_ref, w_ref, b_ref, o_ref):
    c = c_ref[...]
    cond = c * jax.nn.sigmoid(c)
    y = jnp.dot(cond, w_ref[0], preferred_element_type=F32, precision=lax.Precision.HIGHEST)
    one = jnp.where(pl.program_id(1) > 0, 1.0, 0.0).astype(F32)
    o_ref[0] = y + b_ref[0] + one


def _adaln(c, ada_w, ada_b):
    bsz, d = c.shape
    n_sub = ada_w.shape[0] * ada_w.shape[1]
    w = ada_w.reshape(n_sub, d, 3 * d)
    b = ada_b.reshape(n_sub, 1, 3 * d)
    return pl.pallas_call(
        _adaln_kernel,
        grid=(n_sub, 3),
        in_specs=[
            pl.BlockSpec((bsz, d), lambda l, n: (0, 0)),
            pl.BlockSpec((1, d, d), lambda l, n: (l, 0, n)),
            pl.BlockSpec((1, 1, d), lambda l, n: (l, 0, n)),
        ],
        out_specs=pl.BlockSpec((1, bsz, d), lambda l, n: (l, 0, n)),
        out_shape=jax.ShapeDtypeStruct((n_sub, bsz, 3 * d), F32),
        compiler_params=_cparams("arbitrary", "arbitrary"),
        name="adaln",
    )(c, w, b)


def _modulate(x, mod, d):
    return x * mod[:, d:2 * d] + mod[:, :d], mod[:, 2 * d:]


def _proj_ab_kernel(x_ref, mod_ref, w_in_ref, qn_ref, kvn_ref, wq_ref, wk_ref, wv_ref,
                    cq_ref, sq_ref, ck_ref, sk_ref,
                    qa_ref, ka_ref, va_ref, qb_ref, kb_ref, vb_ref):
    d = x_ref.shape[-1]
    u, _ = _modulate(x_ref[0], mod_ref[0], d)
    h = _dot(u.astype(BF16), w_in_ref[...])
    o = 0
    cq = h[:, o:o + Q_LORA]; o += Q_LORA
    ckv = h[:, o:o + KV_LORA]; o += KV_LORA
    kr_a = h[:, o:o + LANES]; o += LANES
    kr_b = h[:, o:o + LANES]; o += LANES
    wb = H_B * LANES
    qb = h[:, o:o + wb]; o += wb
    kb = h[:, o:o + wb]; o += wb
    vb = h[:, o:o + wb]

    nq = _rms_norm(cq, qn_ref[...]).astype(BF16)
    q12 = _dot(nq, wq_ref[...])
    cq_t, sq_t = cq_ref[...], sq_ref[...]
    for hh in range(H_A):
        a = q12[:, hh * LANES:(hh + 1) * LANES]
        b = q12[:, (H_A + hh) * LANES:(H_A + hh + 1) * LANES]
        qa_ref[0, hh] = (a * cq_t + b * sq_t).astype(BF16)

    nkv = _rms_norm(ckv, kvn_ref[...]).astype(BF16)
    kn = _dot(nkv, wk_ref[...])
    vv = _dot(nkv, wv_ref[...])
    k_rope = kr_a * ck_ref[...] + kr_b * sk_ref[...]
    for hh in range(H_A):
        ka_ref[0, hh] = (kn[:, hh * LANES:(hh + 1) * LANES] + k_rope).astype(BF16)
    for p in range(H_A // 2):
        va_ref[0, p] = vv[:, p * LANES:(p + 1) * LANES].astype(BF16)
    scale_b = DH_B ** -0.5 * LOG2E
    for hh in range(H_B):
        sl = slice(hh * LANES, (hh + 1) * LANES)
        qb_ref[0, hh] = (qb[:, sl] * scale_b).astype(BF16)
        kb_ref[0, hh] = kb[:, sl].astype(BF16)
        vb_ref[0, hh] = vb[:, sl].astype(BF16)


def _rope_tables(s_len):
    half = ROPE_DIM // 2
    inv = jnp.power(ROPE_BASE, -jnp.arange(half, dtype=F32) / half)
    ang = jnp.arange(s_len, dtype=F32)[:, None] * inv[None, :]
    cos, sin = jnp.cos(ang), jnp.sin(ang)
    zeros_pad = jnp.zeros((s_len, LANES - NOPE_DIM - ROPE_DIM), F32)
    c_rope = jnp.concatenate([cos, cos], axis=1)
    s_rope = jnp.concatenate([-sin, sin], axis=1)
    cq = jnp.concatenate([jnp.ones((s_len, NOPE_DIM), F32), c_rope, zeros_pad], axis=1)
    ck = jnp.concatenate([jnp.zeros((s_len, NOPE_DIM), F32), c_rope, zeros_pad], axis=1)
    sk = jnp.concatenate([jnp.zeros((s_len, NOPE_DIM), F32), s_rope, zeros_pad], axis=1)
    scale_a = (NOPE_DIM + ROPE_DIM) ** -0.5 * LOG2E
    return cq * scale_a, sk * scale_a, ck, sk


def _swap_halves(w):
    half = w.shape[-1] // 2
    return jnp.concatenate([w[..., half:], w[..., :half]], axis=-1)


def _proj_ab_weights(w_in, w_uq, w_ukv):
    d = w_in.shape[0]
    wb = H_B * 2 * DH_B
    o = 0
    w_cq = w_in[:, o:o + Q_LORA]; o += Q_LORA
    w_ckv = w_in[:, o:o + KV_LORA]; o += KV_LORA
    w_kr = w_in[:, o:o + ROPE_DIM]; o += ROPE_DIM
    w_rest = w_in[:, o:o + 3 * wb]
    lead = jnp.zeros((d, NOPE_DIM), F32)
    tail = jnp.zeros((d, LANES - NOPE_DIM - ROPE_DIM), F32)
    w_kr_a = jnp.concatenate([lead, w_kr, tail], axis=1)
    w_kr_b = jnp.concatenate([lead, _swap_halves(w_kr), tail], axis=1)
    w_in_aug = jnp.concatenate([w_cq, w_ckv, w_kr_a, w_kr_b, w_rest], axis=1).astype(BF16)

    wq = w_uq.reshape(Q_LORA, H_A, NOPE_DIM + ROPE_DIM)
    wq_nope, wq_rope = wq[..., :NOPE_DIM], wq[..., NOPE_DIM:]
    zpad = jnp.zeros((Q_LORA, H_A, LANES - NOPE_DIM - ROPE_DIM), F32)
    wq1 = jnp.concatenate([wq_nope, wq_rope, zpad], axis=-1).reshape(Q_LORA, H_A * LANES)
    wq2 = jnp.concatenate([jnp.zeros_like(wq_nope), _swap_halves(wq_rope), zpad], axis=-1)
    wq12 = jnp.concatenate([wq1, wq2.reshape(Q_LORA, H_A * LANES)], axis=1).astype(BF16)

    wkv = w_ukv.reshape(KV_LORA, H_A, NOPE_DIM + V_DIM_A)
    wk = jnp.concatenate([wkv[..., :NOPE_DIM], jnp.zeros((KV_LORA, H_A, LANES - NOPE_DIM), F32)], axis=-1)
    wk = wk.reshape(KV_LORA, H_A * LANES).astype(BF16)
    wv = wkv[..., NOPE_DIM:].reshape(KV_LORA, H_A * V_DIM_A).astype(BF16)
    return w_in_aug, wq12, wk, wv


def _proj_ab(x, mod, w_in, q_norm, w_uq, kv_norm, w_ukv):
    bsz, s_len, d = x.shape
    tm = min(ROW_TILE, s_len)
    w_in_aug, wq12, wk, wv = _proj_ab_weights(w_in, w_uq, w_ukv)
    tabs = _rope_tables(s_len)
    row = lambda b, t: (b, t, 0)
    head = lambda b, t: (b, 0, t, 0)
    tab_spec = pl.BlockSpec((tm, LANES), lambda b, t: (t, 0))
    out = lambda h: jax.ShapeDtypeStruct((bsz, h, s_len, LANES), BF16)
    out_spec = lambda h: pl.BlockSpec((1, h, tm, LANES), head)
    return pl.pallas_call(
        _proj_ab_kernel,
        grid=(bsz, s_len // tm),
        in_specs=[
            pl.BlockSpec((1, tm, d), row),
            pl.BlockSpec((1, 1, 3 * d), lambda b, t: (b, 0, 0)),
            _const_spec(w_in_aug.shape),
            _const_spec((1, Q_LORA)), _const_spec((1, KV_LORA)),
            _const_spec(wq12.shape), _const_spec(wk.shape), _const_spec(wv.shape),
            tab_spec, tab_spec, tab_spec, tab_spec,
        ],
        out_specs=[out_spec(H_A), out_spec(H_A), out_spec(H_A // 2),
                   out_spec(H_B), out_spec(H_B), out_spec(H_B)],
        out_shape=[out(H_A), out(H_A), out(H_A // 2), out(H_B), out(H_B), out(H_B)],
        compiler_params=_cparams("parallel", "parallel"),
        name="proj_ab",
    )(x, mod, w_in_aug, q_norm.reshape(1, -1), kv_norm.reshape(1, -1), wq12, wk, wv, *tabs)


def _proj_cd_kernel(x_ref, mod_ref, w_ref, wf_ref, qc_ref, kc_ref, vc_ref, qd_ref, kd_ref, vd_ref, fl_ref):
    d = x_ref.shape[-1]
    u, _ = _modulate(x_ref[0], mod_ref[0], d)
    u = u.astype(BF16)
    h = _dot(u, w_ref[...])
    fl_ref[0] = _dot_nt(wf_ref[...], u)
    scale = DH_C ** -0.5 * LOG2E
    n_pair = H_C // 2
    for k, (ref, sc) in enumerate(((qc_ref, scale), (kc_ref, None), (vc_ref, None),
                                   (qd_ref, scale), (kd_ref, None), (vd_ref, None))):
        for p in range(n_pair):
            blk = h[:, (k * n_pair + p) * LANES:(k * n_pair + p + 1) * LANES]
            if sc is not None:
                blk = blk * sc
            ref[0, p] = blk.astype(BF16)


def _proj_cd(x, mod, w_in):
    bsz, s_len, d = x.shape
    tm = min(ROW_TILE, s_len)
    wc, wd = H_C * DH_C, H_D * DH_D
    o = 3 * wc
    w_main = jnp.concatenate([w_in[:, :o], w_in[:, o + H_C:]], axis=1).astype(BF16)
    w_f = w_in[:, o:o + H_C].T.astype(BF16)
    n_pair = H_C // 2
    head = lambda b, t: (b, 0, t, 0)
    out = jax.ShapeDtypeStruct((bsz, n_pair, s_len, LANES), BF16)
    out_spec = pl.BlockSpec((1, n_pair, tm, LANES), head)
    return pl.pallas_call(
        _proj_cd_kernel,
        grid=(bsz, s_len // tm),
        in_specs=[
            pl.BlockSpec((1, tm, d), lambda b, t: (b, t, 0)),
            pl.BlockSpec((1, 1, 3 * d), lambda b, t: (b, 0, 0)),
            _const_spec(w_main.shape), _const_spec(w_f.shape),
        ],
        out_specs=[out_spec] * 6 + [pl.BlockSpec((1, H_C, tm), lambda b, t: (b, 0, t))],
        out_shape=[out] * 6 + [jax.ShapeDtypeStruct((bsz, H_C, s_len), F32)],
        compiler_params=_cparams("parallel", "parallel"),
        name="proj_cd",
    )(x, mod, w_main, w_f)


def _fox_scan_kernel(fl_ref, bf_ref, o_ref):
    z = fl_ref[0] + bf_ref[...]
    x = jnp.minimum(z, 0.0) - jnp.log1p(jnp.exp(-jnp.abs(z)))
    s_len = x.shape[-1]
    pos = lax.broadcasted_iota(jnp.int32, x.shape, 1)
    sh = 1
    while sh < s_len:
        x = x + jnp.where(pos >= sh, pltpu.roll(x, sh, 1), 0.0)
        sh *= 2
    o_ref[0] = -x * LOG2E


def _fox_scan(fl_t, b_f):
    bsz, h, s_len = fl_t.shape
    return pl.pallas_call(
        _fox_scan_kernel,
        grid=(bsz,),
        in_specs=[pl.BlockSpec((1, h, s_len), lambda b: (b, 0, 0)), _const_spec((h, 1))],
        out_specs=pl.BlockSpec((1, h, s_len), lambda b: (b, 0, 0)),
        out_shape=jax.ShapeDtypeStruct((bsz, h, s_len), F32),
        compiler_params=_cparams("parallel"),
        name="fox_scan",
    )(fl_t, b_f.reshape(h, 1))


def _t5_bias_kernel(tab_ref, o_ref):
    h = pl.program_id(0)
    half = T5_BUCKETS // 2
    max_exact = half // 2
    shape = (LANES, 2 * LANES)
    qq = lax.broadcasted_iota(jnp.int32, shape, 0)
    kk = lax.broadcasted_iota(jnp.int32, shape, 1)
    rel = kk - LANES - qq
    n = jnp.abs(rel)
    large = max_exact + (jnp.log(jnp.maximum(n, 1).astype(F32) / max_exact)
                         / math.log(T5_MAX_DIST / max_exact) * (half - max_exact)).astype(jnp.int32)
    large = jnp.minimum(large, half - 1)
    bucket = jnp.where(rel > 0, half, 0) + jnp.where(n < max_exact, n, large)
    val = jnp.zeros(shape, F32)
    for b in range(T5_BUCKETS):
        val = jnp.where(bucket == b, tab_ref[h, b], val)
    o_ref[0] = (val - tab_ref[h, half - 1]) * LOG2E


def _t5_bias(t5_table):
    return pl.pallas_call(
        _t5_bias_kernel,
        grid=(H_B,),
        in_specs=[pl.BlockSpec(memory_space=pltpu.SMEM)],
        out_specs=pl.BlockSpec((1, LANES, 2 * LANES), lambda h: (h, 0, 0)),
        out_shape=jax.ShapeDtypeStruct((H_B, LANES, 2 * LANES), F32),
        compiler_params=_cparams("parallel"),
        name="t5_bias",
    )(t5_table)


def _attn_kernel(*refs, mode, lam_init):
    if mode == "mla":
        q_ref, k_ref, v_ref, o_ref, m_ref, l_ref, acc_ref, s_ref, p_ref, mb_ref = refs
    elif mode == "fox":
        q_ref, k_ref, v_ref, nf_ref, o_ref, m_ref, l_ref, acc_ref, s_ref, p_ref, mb_ref = refs
    else:
        (q_ref, k_ref, v_ref, bias_ref, lq1_ref, lk1_ref, lq2_ref, lk2_ref, subg_ref,
         o_ref, m_ref, l_ref, acc_ref, s_ref, p_ref, mb_ref) = refs
    t = o_ref.shape[1]
    i = pl.program_id(2)
    lane = lax.broadcasted_iota(jnp.int32, (t, LANES), 1)
    low = lane < LANES // 2
    if mode == "mla":
        qs = (q_ref[0, 0], q_ref[0, 1])
    else:
        q = q_ref[0, 0]
        zero = jnp.zeros_like(q)
        qs = (jnp.where(low, q, zero), jnp.where(low, zero, q))

    m_ref[...] = jnp.full(m_ref.shape, NEG_INF, F32)
    l_ref[...] = jnp.zeros(l_ref.shape, F32)
    acc_ref[...] = jnp.zeros(acc_ref.shape, F32)

    def logits(s, j):
        k = k_ref[0, s, j] if mode == "mla" else k_ref[0, 0, j]
        sc = _dot_nt(qs[s], k)
        if mode == "fox":
            sc = sc + nf_ref[0, s, j]
        s_ref[s] = sc

    def softmax_pv(s, j, diag):
        if mode == "diff":
            if diag:
                s_ref[s, :LANES, :LANES] += bias_ref[0, :, LANES:]
                for r in range(1, t // LANES):
                    s_ref[s, r * LANES:(r + 1) * LANES, (r - 1) * LANES:(r + 1) * LANES] += bias_ref[0]
            else:
                is_prev = (j == i - 1).astype(F32)
                s_ref[s, :LANES, t - LANES:] += bias_ref[0, :, :LANES] * is_prev

        def visible_cols(r):
            if not diag:
                return t
            last_row = (r + 1) * STRIP - 1
            last_col = last_row if mode == "fox" else (last_row // CHUNK + 1) * CHUNK - 1
            return (last_col // LANES + 1) * LANES

        m_all, l_all = m_ref[s], l_ref[s]
        m_parts, a_parts, l_parts = [], [], []
        for r in range(t // STRIP):
            rows = slice(r * STRIP, (r + 1) * STRIP)
            ncol = visible_cols(r)
            if diag:
                row = lax.broadcasted_iota(jnp.int32, (STRIP, LANES), 0) + r * STRIP
                col = lax.broadcasted_iota(jnp.int32, (STRIP, LANES), 1) + (ncol - LANES)
                keep = (col <= row) if mode == "fox" else (col // CHUNK) <= (row // CHUNK)
                edge = jnp.where(keep, s_ref[s, rows, ncol - LANES:ncol], NEG_INF)
                s_ref[s, rows, ncol - LANES:ncol] = edge
            blocks = [s_ref[s, rows, c * LANES:(c + 1) * LANES] for c in range(ncol // LANES)]
            mx = functools.reduce(jnp.maximum, blocks)
            m_prev = m_all[rows]
            m_new = jnp.maximum(m_prev, jnp.max(mx, axis=1, keepdims=True))
            m_parts.append(m_new)
            a_parts.append(jnp.exp2(m_prev - m_new))
            mb_ref[s, rows] = jnp.broadcast_to(m_new, (STRIP, LANES))
        for r in range(t // STRIP):
            rows = slice(r * STRIP, (r + 1) * STRIP)
            ncol = visible_cols(r)
            mb = mb_ref[s, rows]
            psum = None
            for c in range(ncol // LANES):
                cols = slice(c * LANES, (c + 1) * LANES)
                p = jnp.exp2(s_ref[s, rows, cols] - mb)
                psum = p if psum is None else psum + p
                p_ref[s, rows, cols] = p.astype(BF16)
            if ncol < t:
                p_ref[s, rows, ncol:] = jnp.zeros((STRIP, t - ncol), BF16)
            l_parts.append(a_parts[r] * l_all[rows] + jnp.sum(psum, axis=1, keepdims=True))
        m_ref[s] = jnp.concatenate(m_parts, axis=0)
        l_ref[s] = jnp.concatenate(l_parts, axis=0)
        acc_ref[s] = jnp.concatenate(a_parts, axis=0) * acc_ref[s] + _dot(p_ref[s], v_ref[0, 0, j])

    logits(0, 0)

    def far_body(j, carry):
        logits(1, j)
        softmax_pv(0, j, False)
        logits(0, j + 1)
        softmax_pv(1, j, False)
        return carry

    lax.fori_loop(0, i, far_body, 0)
    logits(1, i)
    softmax_pv(0, i, True)
    softmax_pv(1, i, True)

    o0 = acc_ref[0] / l_ref[0]
    o1 = acc_ref[1] / l_ref[1]
    if mode == "diff":
        lam = (jnp.exp(jnp.sum(lq1_ref[...] * lk1_ref[...])) - jnp.exp(jnp.sum(lq2_ref[...] * lk2_ref[...]))
               + lam_init)
        o = _rms_norm(o0 - lam * o1, subg_ref[...]) * (1.0 - lam_init)
    else:
        o = jnp.where(low, o0, o1)
    o_ref[0] = o.astype(o_ref.dtype)


def _attention(mode, q, k, v, extra=(), lam_init=0.0):
    bsz, _, s_len, _ = q.shape
    t = min(ATTN_TILE, s_len)
    nk = s_len // t
    n_grp = v.shape[1]
    per = 2 if mode == "mla" else 1
    k5 = k.reshape(bsz, k.shape[1], nk, t, LANES)
    v5 = v.reshape(bsz, n_grp, nk, t, LANES)
    in_specs = [
        pl.BlockSpec((1, per, t, LANES), lambda b, g, i: (b, g, i, 0)),
        pl.BlockSpec((1, per, nk, t, LANES), lambda b, g, i: (b, g, 0, 0, 0)),
        pl.BlockSpec((1, 1, nk, t, LANES), lambda b, g, i: (b, g, 0, 0, 0)),
    ]
    args = [q, k5, v5]
    if mode == "fox":
        (neg_f,) = extra
        args.append(neg_f.reshape(bsz, 2 * n_grp, nk, 1, t))
        in_specs.append(pl.BlockSpec((1, 2, nk, 1, t), lambda b, g, i: (b, g, 0, 0, 0)))
    elif mode == "diff":
        bias, lq1, lk1, lq2, lk2, sub_g = extra
        args += [bias, lq1.reshape(1, -1), lk1.reshape(1, -1), lq2.reshape(1, -1), lk2.reshape(1, -1),
                 sub_g.reshape(1, -1)]
        in_specs.append(pl.BlockSpec((1, LANES, 2 * LANES), lambda b, g, i: (g, 0, 0)))
        in_specs += [_const_spec((1, DH_B))] * 4 + [_const_spec((1, 2 * DH_B))]
    scratch = [pltpu.VMEM((2, t, 1), F32), pltpu.VMEM((2, t, 1), F32), pltpu.VMEM((2, t, LANES), F32),
               pltpu.VMEM((2, t, t), F32), pltpu.VMEM((2, t, t), BF16), pltpu.VMEM((2, t, LANES), F32)]
    return pl.pallas_call(
        functools.partial(_attn_kernel, mode=mode, lam_init=lam_init),
        grid=(bsz, n_grp, s_len // t),
        in_specs=in_specs,
        out_specs=pl.BlockSpec((1, t, LANES), lambda b, g, i: (b, i, g)),
        out_shape=jax.ShapeDtypeStruct((bsz, s_len, n_grp * LANES), BF16),
        scratch_shapes=scratch,
        compiler_params=_cparams("parallel", "parallel", "arbitrary"),
        name="attn_" + mode,
    )(*args)


def _band_bias_kernel(e_ref, o_ref):
    rows, win = o_ref.shape[1], o_ref.shape[2]
    ext = jnp.broadcast_to(e_ref[0], (rows, e_ref.shape[-1]))
    o_ref[0] = pltpu.roll(ext, win + 1, 1, stride=1, stride_axis=0)[:, :win] * LOG2E


def _band_bias(rel_table):
    ext_len = BAND_WIN + BAND_GROUP
    flipped = rel_table[:, ::-1]
    left = BAND_LEAD - REL_CLIP + BAND_GROUP - 1
    ext = jnp.pad(flipped, ((0, 0), (left, ext_len - left - flipped.shape[1])), mode="edge")
    return pl.pallas_call(
        _band_bias_kernel,
        grid=(H_D,),
        in_specs=[pl.BlockSpec((1, 1, ext_len), lambda h: (h, 0, 0))],
        out_specs=pl.BlockSpec((1, BAND_GROUP, BAND_WIN), lambda h: (h, 0, 0)),
        out_shape=jax.ShapeDtypeStruct((H_D, BAND_GROUP, BAND_WIN), F32),
        compiler_params=_cparams("parallel"),
        name="band_bias",
    )(ext.reshape(H_D, 1, ext_len))


def _band_mask_kernel(o_ref):
    row = lax.broadcasted_iota(jnp.int32, o_ref.shape, 0) % BAND_GROUP
    col = lax.broadcasted_iota(jnp.int32, o_ref.shape, 1)
    first = (row // CHUNK) * CHUNK
    in_band = (col >= first) & (col < first + BAND_LEAD + CHUNK)
    o_ref[...] = jnp.where(in_band, col, -BAND_WIN * 2)


def _band_mask():
    shape = (2 * BAND_GROUP, BAND_WIN)
    return pl.pallas_call(
        _band_mask_kernel,
        out_specs=pl.BlockSpec(shape, lambda: (0, 0)),
        out_shape=jax.ShapeDtypeStruct(shape, jnp.int32),
        compiler_params=pltpu.CompilerParams(vmem_limit_bytes=VMEM_LIMIT),
        name="band_mask",
    )()


def _band_kernel(q_ref, kp_ref, kc_ref, vp_ref, vc_ref, bias_ref, vis_ref, o_ref):
    t = o_ref.shape[1]
    i = pl.program_id(2)
    lane = lax.broadcasted_iota(jnp.int32, (BAND_GROUP, LANES), 1)
    low = lane < LANES // 2
    kw = jnp.concatenate([kp_ref[0, 0], kc_ref[0, 0]], axis=0)
    vw = jnp.concatenate([vp_ref[0, 0], vc_ref[0, 0]], axis=0)
    for r in range(t // BAND_GROUP):
        q = q_ref[0, 0, r * BAND_GROUP:(r + 1) * BAND_GROUP]
        zero = jnp.zeros_like(q)
        q2 = jnp.concatenate([jnp.where(low, q, zero), jnp.where(low, zero, q)], axis=0)
        w0 = t - BAND_LEAD + r * BAND_GROUP
        keep = vis_ref[...] >= jnp.maximum(BAND_LEAD - r * BAND_GROUP - i * t, 0)
        sc = _dot_nt(q2, kw[w0:w0 + BAND_WIN]) + bias_ref[0]
        sc = jnp.where(keep, sc, NEG_INF)
        p = jnp.exp2(sc - jnp.max(sc, axis=1, keepdims=True))
        denom = jnp.sum(p, axis=1, keepdims=True)
        o2 = _dot(p.astype(BF16), vw[w0:w0 + BAND_WIN]) / denom
        o = jnp.where(low, o2[:BAND_GROUP], o2[BAND_GROUP:])
        o_ref[0, r * BAND_GROUP:(r + 1) * BAND_GROUP] = o.astype(o_ref.dtype)


def _band_attention(q, k, v, bias):
    bsz, n_grp, s_len, _ = q.shape
    t = min(BAND_TILE, s_len)
    assert t >= BAND_LEAD
    cur = lambda b, g, i: (b, g, i, 0)
    prev = lambda b, g, i: (b, g, jnp.maximum(i - 1, 0), 0)
    blk = (1, 1, t, LANES)
    return pl.pallas_call(
        _band_kernel,
        grid=(bsz, n_grp, s_len // t),
        in_specs=[pl.BlockSpec(blk, cur), pl.BlockSpec(blk, prev), pl.BlockSpec(blk, cur),
                  pl.BlockSpec(blk, prev), pl.BlockSpec(blk, cur),
                  pl.BlockSpec((1, 2 * BAND_GROUP, BAND_WIN), lambda b, g, i: (g, 0, 0)),
                  _const_spec((2 * BAND_GROUP, BAND_WIN))],
        out_specs=pl.BlockSpec((1, t, LANES), lambda b, g, i: (b, i, g)),
        out_shape=jax.ShapeDtypeStruct((bsz, s_len, n_grp * LANES), BF16),
        compiler_params=_cparams("parallel", "parallel", "parallel"),
        name="attn_band",
    )(q, k, k, v, v, bias.reshape(n_grp, 2 * BAND_GROUP, BAND_WIN), _band_mask())


def _out_ln_kernel(oa_ref, ob_ref, w_ref, x_ref, mod_ref, g_ref, b_ref, o_ref):
    d = x_ref.shape[-1]
    half = oa_ref.shape[-1]
    y = _dot(oa_ref[0], w_ref[:half]) + _dot(ob_ref[0], w_ref[half:])
    gate = mod_ref[0][:, 2 * d:]
    o_ref[0] = _layer_norm(DEEPNORM_ALPHA * x_ref[0] + gate * y, g_ref[...], b_ref[...])


def _out_ln(oa, ob, w_out, x, mod, ln_g, ln_b):
    bsz, s_len, d = x.shape
    tm = min(ROW_TILE, s_len)
    row = lambda b, t: (b, t, 0)
    return pl.pallas_call(
        _out_ln_kernel,
        grid=(bsz, s_len // tm),
        in_specs=[
            pl.BlockSpec((1, tm, oa.shape[-1]), row), pl.BlockSpec((1, tm, ob.shape[-1]), row),
            _const_spec(w_out.shape),
            pl.BlockSpec((1, tm, d), row),
            pl.BlockSpec((1, 1, 3 * d), lambda b, t: (b, 0, 0)),
            _const_spec((1, d)), _const_spec((1, d)),
        ],
        out_specs=pl.BlockSpec((1, tm, d), row),
        out_shape=jax.ShapeDtypeStruct((bsz, s_len, d), F32),
        compiler_params=_cparams("parallel", "parallel"),
        name="out_ln",
    )(oa, ob, w_out.astype(BF16), x, mod, ln_g.reshape(1, d), ln_b.reshape(1, d))


def _ffn_kernel(x_ref, mod_ref, wg_ref, wv_ref, cw_ref, cb_ref, wd_ref, g_ref, b_ref, o_ref, carry_ref):
    d = x_ref.shape[-1]
    tm = x_ref.shape[1]
    x = x_ref[0]
    u, gate = _modulate(x, mod_ref[0], d)
    u = u.astype(BF16)
    d_ff = wg_ref.shape[1]
    n_tiles = d_ff // MXU_COLS
    edges = [(c * n_tiles // FFN_COL_CHUNKS) * MXU_COLS for c in range(FFN_COL_CHUNKS)] + [d_ff]
    seq_start = pl.program_id(1) == 0
    y = None
    for c in range(FFN_COL_CHUNKS):
        cols = slice(edges[c], edges[c + 1])
        row = lax.broadcasted_iota(jnp.int32, (tm, edges[c + 1] - edges[c]), 0)
        g = _dot(u, wg_ref[:, cols])
        val = _dot(u, wv_ref[:, cols])
        prev = jnp.where(seq_start, 0.0, carry_ref[:, cols])
        g1 = jnp.where(row == 0, prev[7:8], pltpu.roll(g, 1, 0))
        g2 = jnp.where(row == 0, prev[6:7], jnp.where(row == 1, prev[7:8], pltpu.roll(g, 2, 0)))
        carry_ref[:, cols] = g[tm - 8:]
        cw = cw_ref[:, cols]
        gc = cw[0:1] * g2 + cw[1:2] * g1 + cw[2:3] * g + cb_ref[:, cols]
        hmid = (gc * jax.nn.sigmoid(gc) * val).astype(BF16)
        yc = _dot(hmid, wd_ref[cols, :])
        y = yc if y is None else y + yc
    o_ref[0] = _layer_norm(DEEPNORM_ALPHA * x + gate * y, g_ref[...], b_ref[...])


def _ffn(x, mod, w_gate, w_val, conv_w, conv_b, w_down, ln_g, ln_b):
    bsz, s_len, d = x.shape
    d_ff = w_gate.shape[1]
    tm = min(FFN_ROW_TILE, s_len)
    row = lambda b, t: (b, t, 0)
    once = lambda shape: pl.BlockSpec(shape, lambda b, t: (0,) * len(shape), pipeline_mode=pl.Buffered(1))
    return pl.pallas_call(
        _ffn_kernel,
        grid=(bsz, s_len // tm),
        in_specs=[
            pl.BlockSpec((1, tm, d), row),
            pl.BlockSpec((1, 1, 3 * d), lambda b, t: (b, 0, 0)),
            once((d, d_ff)), once((d, d_ff)),
            _const_spec((CONV_W, d_ff)), _const_spec((1, d_ff)),
            once((d_ff, d)),
            _const_spec((1, d)), _const_spec((1, d)),
        ],
        out_specs=pl.BlockSpec((1, tm, d), row),
        out_shape=jax.ShapeDtypeStruct((bsz, s_len, d), F32),
        scratch_shapes=[pltpu.VMEM((8, d_ff), F32)],
        compiler_params=_cparams("parallel", "arbitrary"),
        name="ffn",
    )(x, mod, w_gate.astype(BF16), w_val.astype(BF16), conv_w, conv_b.reshape(1, d_ff),
      w_down.astype(BF16), ln_g.reshape(1, d), ln_b.reshape(1, d))


def kernel(x, c, ada_w, ada_b, ln_g, ln_b, t5_table, ab_w_in, mla_q_norm, mla_w_uq, mla_kv_norm, mla_w_ukv, diff_lq1, diff_lk1, diff_lq2, diff_lk2, diff_sub_g, ab_w_out, cd_w_in, fox_b_f, chunk_rel_table, cd_w_out, ffn_w_gate, ffn_w_val, ffn_conv_w, ffn_conv_b, ffn_w_down):
    bsz, s_len, d = x.shape
    depth = ada_w.shape[0]
    mods = _adaln(c, ada_w, ada_b).reshape(depth, 2, bsz, 1, 3 * d)
    for i in range(depth):
        mod = mods[i, 0]
        if i % 2 == 0:
            e = i // 2
            qa, ka, va, qb, kb, vb = _proj_ab(x, mod, ab_w_in[e], mla_q_norm[e], mla_w_uq[e],
                                              mla_kv_norm[e], mla_w_ukv[e])
            o_first = _attention("mla", qa, ka, va)
            lam_init = 0.8 - 0.6 * math.exp(-0.3 * i)
            o_second = _attention("diff", qb, kb, vb,
                                  extra=(_t5_bias(t5_table), diff_lq1[e], diff_lk1[e], diff_lq2[e],
                                         diff_lk2[e], diff_sub_g[e]),
                                  lam_init=lam_init)
            w_out = ab_w_out[e]
        else:
            o = i // 2
            qc, kc, vc, qd, kd, vd, fl_t = _proj_cd(x, mod, cd_w_in[o])
            neg_f = _fox_scan(fl_t, fox_b_f[o])
            o_first = _attention("fox", qc, kc, vc, extra=(neg_f,))
            o_second = _band_attention(qd, kd, vd, _band_bias(chunk_rel_table[o]))
            w_out = cd_w_out[o]
        x = _out_ln(o_first, o_second, w_out, x, mod, ln_g[i, 0], ln_b[i, 0])
        x = _ffn(x, mods[i, 1], ffn_w_gate[i], ffn_w_val[i], ffn_conv_w[i], ffn_conv_b[i], ffn_w_down[i],
                 ln_g[i, 1], ln_b[i, 1])
    return x
```

```python
import functools
import math

import jax
import jax.numpy as jnp
from jax import lax
from jax.experimental import pallas as pl
from jax.experimental.pallas import tpu as pltpu

DEPTH = 2
CHUNK = 64
H_A, Q_LORA, KV_LORA, NOPE_DIM, ROPE_DIM, V_DIM_A = 8, 256, 128, 64, 32, 64
ROPE_BASE = 10000.0
H_B, DH_B, T5_BUCKETS, T5_MAX_DIST = 4, 64, 32, 128
H_C, DH_C = 8, 64
H_D, DH_D, BAND_CHUNKS, REL_CLIP = 8, 64, 8, 128
CONV_W = 3
EPS_LN = 1e-5
EPS_RMS = 1e-6
DEEPNORM_ALPHA = (2 * DEPTH) ** 0.25
NEG_INF = -1e30
LOG2E = math.log2(math.e)

MXU_COLS = 256
LANES = 128
ATTN_TILE = 1024
STRIP = 64
BAND_TILE = 512
ROW_TILE = 512
FFN_ROW_TILE = 512
FFN_COL_CHUNKS = 2
BAND_GROUP = 2 * CHUNK
BAND_LEAD = BAND_CHUNKS * CHUNK
BAND_WIN = BAND_LEAD + BAND_GROUP
VMEM_LIMIT = 56 * 1024 * 1024

BF16 = jnp.bfloat16
F32 = jnp.float32


def _cparams(*sem, flags=None):
    return pltpu.CompilerParams(dimension_semantics=sem, vmem_limit_bytes=VMEM_LIMIT, flags=flags)


def _dot(a, b):
    return jnp.dot(a, b, preferred_element_type=F32)


def _dot_nt(a, b):
    return lax.dot_general(a, b, (((1,), (1,)), ((), ())), preferred_element_type=F32)


def _layer_norm(z, g, b):
    mu = jnp.mean(z, axis=-1, keepdims=True)
    zc = z - mu
    var = jnp.mean(zc * zc, axis=-1, keepdims=True)
    return zc * lax.rsqrt(var + EPS_LN) * g + b


def _rms_norm(z, g):
    ms = jnp.mean(z * z, axis=-1, keepdims=True)
    return z * lax.rsqrt(ms + EPS_RMS) * g


def _ones_column(rows):
    lane = lax.broadcasted_iota(jnp.int32, (rows, LANES), 1)
    return jnp.where(lane == 0, 1.0, 0.0).astype(BF16)


def _const_spec(shape):
    return pl.BlockSpec(shape, lambda *_: (0,) * len(shape))


def _adaln_kernel(c_ref, w_ref, b_ref, o_ref):
    c = c_ref[...]
    cond = c * jax.nn.sigmoid(c)
    y = jnp.dot(cond, w_ref[0], preferred_element_type=F32, precision=lax.Precision.HIGHEST)
    one = jnp.where(pl.program_id(1) > 0, 1.0, 0.0).astype(F32)
    o_ref[0] = y + b_ref[0] + one


def _adaln(c, ada_w, ada_b):
    bsz, d = c.shape
    n_sub = ada_w.shape[0] * ada_w.shape[1]
    w = ada_w.reshape(n_sub, d, 3 * d)
    b = ada_b.reshape(n_sub, 1, 3 * d)
    return pl.pallas_call(
        _adaln_kernel,
        grid=(n_sub, 3),
        in_specs=[
            pl.BlockSpec((bsz, d), lambda l, n: (0, 0)),
            pl.BlockSpec((1, d, d), lambda l, n: (l, 0, n)),
            pl.BlockSpec((1, 1, d), lambda l, n: (l, 0, n)),
        ],
        out_specs=pl.BlockSpec((1, bsz, d), lambda l, n: (l, 0, n)),
        out_shape=jax.ShapeDtypeStruct((n_sub, bsz, 3 * d), F32),
        compiler_params=_cparams("arbitrary", "arbitrary"),
        name="adaln",
    )(c, w, b)


def _modulate(x, mod, d):
    return x * mod[:, d:2 * d] + mod[:, :d], mod[:, 2 * d:]


def _proj_ab_kernel(x_ref, mod_ref, w_in_ref, qn_ref, kvn_ref, wq_ref, wk_ref, wv_ref,
                    cq_ref, sq_ref, ck_ref, sk_ref,
                    qa_ref, ka_ref, va_ref, qb_ref, kb_ref, vb_ref):
    d = x_ref.shape[-1]
    u, _ = _modulate(x_ref[0], mod_ref[0], d)
    h = _dot(u.astype(BF16), w_in_ref[...])
    o = 0
    cq = h[:, o:o + Q_LORA]; o += Q_LORA
    ckv = h[:, o:o + KV_LORA]; o += KV_LORA
    kr_a = h[:, o:o + LANES]; o += LANES
    kr_b = h[:, o:o + LANES]; o += LANES
    wb = H_B * LANES
    qb = h[:, o:o + wb]; o += wb
    kb = h[:, o:o + wb]; o += wb
    vb = h[:, o:o + wb]

    nq = _rms_norm(cq, qn_ref[...]).astype(BF16)
    q12 = _dot(nq, wq_ref[...])
    cq_t, sq_t = cq_ref[...], sq_ref[...]
    for hh in range(H_A):
        a = q12[:, hh * LANES:(hh + 1) * LANES]
        b = q12[:, (H_A + hh) * LANES:(H_A + hh + 1) * LANES]
        qa_ref[0, hh] = (a * cq_t + b * sq_t).astype(BF16)

    nkv = _rms_norm(ckv, kvn_ref[...]).astype(BF16)
    kn = _dot(nkv, wk_ref[...])
    vv = _dot(nkv, wv_ref[...])
    k_rope = kr_a * ck_ref[...] + kr_b * sk_ref[...]
    for hh in range(H_A):
        ka_ref[0, hh] = (kn[:, hh * LANES:(hh + 1) * LANES] + k_rope).astype(BF16)
    ones_col = _ones_column(vv.shape[0])
    for p in range(H_A // 2):
        va_ref[0, p, :, :LANES] = vv[:, p * LANES:(p + 1) * LANES].astype(BF16)
        va_ref[0, p, :, LANES:] = ones_col
    scale_b = DH_B ** -0.5 * LOG2E
    for hh in range(H_B):
        sl = slice(hh * LANES, (hh + 1) * LANES)
        qb_ref[0, hh] = (qb[:, sl] * scale_b).astype(BF16)
        kb_ref[0, hh] = kb[:, sl].astype(BF16)
        vb_ref[0, hh, :, :LANES] = vb[:, sl].astype(BF16)
        vb_ref[0, hh, :, LANES:] = ones_col


def _rope_tables(s_len):
    half = ROPE_DIM // 2
    inv = jnp.power(ROPE_BASE, -jnp.arange(half, dtype=F32) / half)
    ang = jnp.arange(s_len, dtype=F32)[:, None] * inv[None, :]
    cos, sin = jnp.cos(ang), jnp.sin(ang)
    zeros_pad = jnp.zeros((s_len, LANES - NOPE_DIM - ROPE_DIM), F32)
    c_rope = jnp.concatenate([cos, cos], axis=1)
    s_rope = jnp.concatenate([-sin, sin], axis=1)
    cq = jnp.concatenate([jnp.ones((s_len, NOPE_DIM), F32), c_rope, zeros_pad], axis=1)
    ck = jnp.concatenate([jnp.zeros((s_len, NOPE_DIM), F32), c_rope, zeros_pad], axis=1)
    sk = jnp.concatenate([jnp.zeros((s_len, NOPE_DIM), F32), s_rope, zeros_pad], axis=1)
    scale_a = (NOPE_DIM + ROPE_DIM) ** -0.5 * LOG2E
    return cq * scale_a, sk * scale_a, ck, sk


def _swap_halves(w):
    half = w.shape[-1] // 2
    return jnp.concatenate([w[..., half:], w[..., :half]], axis=-1)


def _proj_ab_weights(w_in, w_uq, w_ukv):
    d = w_in.shape[0]
    wb = H_B * 2 * DH_B
    o = 0
    w_cq = w_in[:, o:o + Q_LORA]; o += Q_LORA
    w_ckv = w_in[:, o:o + KV_LORA]; o += KV_LORA
    w_kr = w_in[:, o:o + ROPE_DIM]; o += ROPE_DIM
    w_rest = w_in[:, o:o + 3 * wb]
    lead = jnp.zeros((d, NOPE_DIM), F32)
    tail = jnp.zeros((d, LANES - NOPE_DIM - ROPE_DIM), F32)
    w_kr_a = jnp.concatenate([lead, w_kr, tail], axis=1)
    w_kr_b = jnp.concatenate([lead, _swap_halves(w_kr), tail], axis=1)
    w_in_aug = jnp.concatenate([w_cq, w_ckv, w_kr_a, w_kr_b, w_rest], axis=1).astype(BF16)

    wq = w_uq.reshape(Q_LORA, H_A, NOPE_DIM + ROPE_DIM)
    wq_nope, wq_rope = wq[..., :NOPE_DIM], wq[..., NOPE_DIM:]
    zpad = jnp.zeros((Q_LORA, H_A, LANES - NOPE_DIM - ROPE_DIM), F32)
    wq1 = jnp.concatenate([wq_nope, wq_rope, zpad], axis=-1).reshape(Q_LORA, H_A * LANES)
    wq2 = jnp.concatenate([jnp.zeros_like(wq_nope), _swap_halves(wq_rope), zpad], axis=-1)
    wq12 = jnp.concatenate([wq1, wq2.reshape(Q_LORA, H_A * LANES)], axis=1).astype(BF16)

    wkv = w_ukv.reshape(KV_LORA, H_A, NOPE_DIM + V_DIM_A)
    wk = jnp.concatenate([wkv[..., :NOPE_DIM], jnp.zeros((KV_LORA, H_A, LANES - NOPE_DIM), F32)], axis=-1)
    wk = wk.reshape(KV_LORA, H_A * LANES).astype(BF16)
    wv = wkv[..., NOPE_DIM:].reshape(KV_LORA, H_A * V_DIM_A).astype(BF16)
    return w_in_aug, wq12, wk, wv


def _proj_ab(x, mod, w_in, q_norm, w_uq, kv_norm, w_ukv):
    bsz, s_len, d = x.shape
    tm = min(ROW_TILE, s_len)
    w_in_aug, wq12, wk, wv = _proj_ab_weights(w_in, w_uq, w_ukv)
    tabs = _rope_tables(s_len)
    row = lambda b, t: (b, t, 0)
    head = lambda b, t: (b, 0, t, 0)
    tab_spec = pl.BlockSpec((tm, LANES), lambda b, t: (t, 0))
    out = lambda h, w=LANES: jax.ShapeDtypeStruct((bsz, h, s_len, w), BF16)
    out_spec = lambda h, w=LANES: pl.BlockSpec((1, h, tm, w), head)
    return pl.pallas_call(
        _proj_ab_kernel,
        grid=(bsz, s_len // tm),
        in_specs=[
            pl.BlockSpec((1, tm, d), row),
            pl.BlockSpec((1, 1, 3 * d), lambda b, t: (b, 0, 0)),
            _const_spec(w_in_aug.shape),
            _const_spec((1, Q_LORA)), _const_spec((1, KV_LORA)),
            _const_spec(wq12.shape), _const_spec(wk.shape), _const_spec(wv.shape),
            tab_spec, tab_spec, tab_spec, tab_spec,
        ],
        out_specs=[out_spec(H_A), out_spec(H_A), out_spec(H_A // 2, 2 * LANES),
                   out_spec(H_B), out_spec(H_B), out_spec(H_B, 2 * LANES)],
        out_shape=[out(H_A), out(H_A), out(H_A // 2, 2 * LANES), out(H_B), out(H_B), out(H_B, 2 * LANES)],
        compiler_params=_cparams("parallel", "parallel"),
        name="proj_ab",
    )(x, mod, w_in_aug, q_norm.reshape(1, -1), kv_norm.reshape(1, -1), wq12, wk, wv, *tabs)


def _proj_cd_kernel(x_ref, mod_ref, w_ref, wf_ref, qc_ref, kc_ref, vc_ref, qd_ref, kd_ref, vd_ref, fl_ref):
    d = x_ref.shape[-1]
    u, _ = _modulate(x_ref[0], mod_ref[0], d)
    u = u.astype(BF16)
    h = _dot(u, w_ref[...])
    fl_ref[0] = _dot_nt(wf_ref[...], u)
    scale = DH_C ** -0.5 * LOG2E
    n_pair = H_C // 2
    for k, (ref, sc) in enumerate(((qc_ref, scale), (kc_ref, None), (vc_ref, None),
                                   (qd_ref, scale), (kd_ref, None), (vd_ref, None))):
        for p in range(n_pair):
            blk = h[:, (k * n_pair + p) * LANES:(k * n_pair + p + 1) * LANES]
            if sc is not None:
                blk = blk * sc
            ref[0, p, :, :LANES] = blk.astype(BF16)
            if ref is vc_ref or ref is vd_ref:
                ref[0, p, :, LANES:] = _ones_column(blk.shape[0])


def _proj_cd(x, mod, w_in):
    bsz, s_len, d = x.shape
    tm = min(ROW_TILE, s_len)
    wc, wd = H_C * DH_C, H_D * DH_D
    o = 3 * wc
    w_main = jnp.concatenate([w_in[:, :o], w_in[:, o + H_C:]], axis=1).astype(BF16)
    w_f = w_in[:, o:o + H_C].T.astype(BF16)
    n_pair = H_C // 2
    head = lambda b, t: (b, 0, t, 0)
    out = lambda w: jax.ShapeDtypeStruct((bsz, n_pair, s_len, w), BF16)
    out_spec = lambda w: pl.BlockSpec((1, n_pair, tm, w), head)
    widths = (LANES, LANES, 2 * LANES, LANES, LANES, 2 * LANES)
    return pl.pallas_call(
        _proj_cd_kernel,
        grid=(bsz, s_len // tm),
        in_specs=[
            pl.BlockSpec((1, tm, d), lambda b, t: (b, t, 0)),
            pl.BlockSpec((1, 1, 3 * d), lambda b, t: (b, 0, 0)),
            _const_spec(w_main.shape), _const_spec(w_f.shape),
        ],
        out_specs=[out_spec(w) for w in widths] + [pl.BlockSpec((1, H_C, tm), lambda b, t: (b, 0, t))],
        out_shape=[out(w) for w in widths] + [jax.ShapeDtypeStruct((bsz, H_C, s_len), F32)],
        compiler_params=_cparams("parallel", "parallel"),
        name="proj_cd",
    )(x, mod, w_main, w_f)


def _fox_scan_kernel(fl_ref, bf_ref, o_ref):
    z = fl_ref[0] + bf_ref[...]
    x = jnp.minimum(z, 0.0) - jnp.log1p(jnp.exp(-jnp.abs(z)))
    s_len = x.shape[-1]
    pos = lax.broadcasted_iota(jnp.int32, x.shape, 1)
    sh = 1
    while sh < s_len:
        x = x + jnp.where(pos >= sh, pltpu.roll(x, sh, 1), 0.0)
        sh *= 2
    o_ref[0] = -x * LOG2E


def _fox_scan(fl_t, b_f):
    bsz, h, s_len = fl_t.shape
    return pl.pallas_call(
        _fox_scan_kernel,
        grid=(bsz,),
        in_specs=[pl.BlockSpec((1, h, s_len), lambda b: (b, 0, 0)), _const_spec((h, 1))],
        out_specs=pl.BlockSpec((1, h, s_len), lambda b: (b, 0, 0)),
        out_shape=jax.ShapeDtypeStruct((bsz, h, s_len), F32),
        compiler_params=_cparams("parallel"),
        name="fox_scan",
    )(fl_t, b_f.reshape(h, 1))


def _t5_bias_kernel(tab_ref, o_ref):
    h = pl.program_id(0)
    half = T5_BUCKETS // 2
    max_exact = half // 2
    shape = (LANES, 2 * LANES)
    qq = lax.broadcasted_iota(jnp.int32, shape, 0)
    kk = lax.broadcasted_iota(jnp.int32, shape, 1)
    rel = kk - LANES - qq
    n = jnp.abs(rel)
    large = max_exact + (jnp.log(jnp.maximum(n, 1).astype(F32) / max_exact)
                         / math.log(T5_MAX_DIST / max_exact) * (half - max_exact)).astype(jnp.int32)
    large = jnp.minimum(large, half - 1)
    bucket = jnp.where(rel > 0, half, 0) + jnp.where(n < max_exact, n, large)
    val = jnp.zeros(shape, F32)
    for b in range(T5_BUCKETS):
        val = jnp.where(bucket == b, tab_ref[h, b], val)
    o_ref[0] = (val - tab_ref[h, half - 1]) * LOG2E


def _t5_bias(t5_table):
    return pl.pallas_call(
        _t5_bias_kernel,
        grid=(H_B,),
        in_specs=[pl.BlockSpec(memory_space=pltpu.SMEM)],
        out_specs=pl.BlockSpec((1, LANES, 2 * LANES), lambda h: (h, 0, 0)),
        out_shape=jax.ShapeDtypeStruct((H_B, LANES, 2 * LANES), F32),
        compiler_params=_cparams("parallel"),
        name="t5_bias",
    )(t5_table)


def _attn_kernel(*refs, mode, lam_init):
    if mode == "mla":
        q_ref, k_ref, v_ref, o_ref, m_ref, acc_ref, s_ref, p_ref, mb_ref = refs
    elif mode == "fox":
        q_ref, k_ref, v_ref, nf_ref, o_ref, m_ref, acc_ref, s_ref, p_ref, mb_ref = refs
    else:
        (q_ref, k_ref, v_ref, bias_ref, lq1_ref, lk1_ref, lq2_ref, lk2_ref, subg_ref,
         o_ref, m_ref, acc_ref, s_ref, p_ref, mb_ref) = refs
    t = o_ref.shape[1]
    i = pl.program_id(2)
    lane = lax.broadcasted_iota(jnp.int32, (t, LANES), 1)
    low = lane < LANES // 2
    if mode == "mla":
        qs = (q_ref[0, 0], q_ref[0, 1])
    else:
        q = q_ref[0, 0]
        zero = jnp.zeros_like(q)
        qs = (jnp.where(low, q, zero), jnp.where(low, zero, q))

    m_ref[...] = jnp.full(m_ref.shape, NEG_INF, F32)
    acc_ref[...] = jnp.zeros(acc_ref.shape, F32)

    def logits(s, j):
        k = k_ref[0, s, j] if mode == "mla" else k_ref[0, 0, j]
        sc = _dot_nt(qs[s], k)
        if mode == "fox":
            sc = sc + nf_ref[0, s, j]
        s_ref[s] = sc

    def softmax_pv(s, j, diag):
        if mode == "diff":
            if diag:
                s_ref[s, :LANES, :LANES] += bias_ref[0, :, LANES:]
                for r in range(1, t // LANES):
                    s_ref[s, r * LANES:(r + 1) * LANES, (r - 1) * LANES:(r + 1) * LANES] += bias_ref[0]
            else:
                is_prev = (j == i - 1).astype(F32)
                s_ref[s, :LANES, t - LANES:] += bias_ref[0, :, :LANES] * is_prev

        def visible_cols(r):
            if not diag:
                return t
            last_row = (r + 1) * STRIP - 1
            last_col = last_row if mode == "fox" else (last_row // CHUNK + 1) * CHUNK - 1
            return (last_col // LANES + 1) * LANES

        m_all = m_ref[s]
        m_parts, a_parts = [], []
        for r in range(t // STRIP):
            rows = slice(r * STRIP, (r + 1) * STRIP)
            ncol = visible_cols(r)
            if diag:
                row = lax.broadcasted_iota(jnp.int32, (STRIP, LANES), 0) + r * STRIP
                col = lax.broadcasted_iota(jnp.int32, (STRIP, LANES), 1) + (ncol - LANES)
                keep = (col <= row) if mode == "fox" else (col // CHUNK) <= (row // CHUNK)
                edge = jnp.where(keep, s_ref[s, rows, ncol - LANES:ncol], NEG_INF)
                s_ref[s, rows, ncol - LANES:ncol] = edge
            blocks = [s_ref[s, rows, c * LANES:(c + 1) * LANES] for c in range(ncol // LANES)]
            mx = functools.reduce(jnp.maximum, blocks)
            m_prev = m_all[rows]
            m_new = jnp.maximum(m_prev, jnp.max(mx, axis=1, keepdims=True))
            m_parts.append(m_new)
            a_parts.append(jnp.exp2(m_prev - m_new))
            mb_ref[s, rows] = jnp.broadcast_to(m_new, (STRIP, LANES))
        for r in range(t // STRIP):
            rows = slice(r * STRIP, (r + 1) * STRIP)
            ncol = visible_cols(r)
            mb = mb_ref[s, rows]
            for c in range(ncol // LANES):
                cols = slice(c * LANES, (c + 1) * LANES)
                p_ref[s, rows, cols] = jnp.exp2((s_ref[s, rows, cols] - mb).astype(BF16))
            if ncol < t:
                p_ref[s, rows, ncol:] = jnp.zeros((STRIP, t - ncol), BF16)
        m_ref[s] = jnp.concatenate(m_parts, axis=0)
        acc_ref[s] = jnp.concatenate(a_parts, axis=0) * acc_ref[s] + _dot(p_ref[s], v_ref[0, 0, j])

    logits(0, 0)

    def far_body(j, carry):
        logits(1, j)
        softmax_pv(0, j, False)
        logits(0, j + 1)
        softmax_pv(1, j, False)
        return carry

    lax.fori_loop(0, i, far_body, 0)
    logits(1, i)
    softmax_pv(0, i, True)
    softmax_pv(1, i, True)

    o0 = acc_ref[0, :, :LANES] / acc_ref[0, :, LANES:LANES + 1]
    o1 = acc_ref[1, :, :LANES] / acc_ref[1, :, LANES:LANES + 1]
    if mode == "diff":
        lam = (jnp.exp(jnp.sum(lq1_ref[...] * lk1_ref[...])) - jnp.exp(jnp.sum(lq2_ref[...] * lk2_ref[...]))
               + lam_init)
        o = _rms_norm(o0 - lam * o1, subg_ref[...]) * (1.0 - lam_init)
    else:
        o = jnp.where(low, o0, o1)
    o_ref[0] = o.astype(o_ref.dtype)


def _attention(mode, q, k, v, extra=(), lam_init=0.0):
    bsz, _, s_len, _ = q.shape
    t = min(ATTN_TILE, s_len)
    nk = s_len // t
    n_grp = v.shape[1]
    per = 2 if mode == "mla" else 1
    k5 = k.reshape(bsz, k.shape[1], nk, t, LANES)
    v5 = v.reshape(bsz, n_grp, nk, t, 2 * LANES)
    in_specs = [
        pl.BlockSpec((1, per, t, LANES), lambda b, g, i: (b, g, i, 0)),
        pl.BlockSpec((1, per, nk, t, LANES), lambda b, g, i: (b, g, 0, 0, 0)),
        pl.BlockSpec((1, 1, nk, t, 2 * LANES), lambda b, g, i: (b, g, 0, 0, 0)),
    ]
    args = [q, k5, v5]
    if mode == "fox":
        (neg_f,) = extra
        args.append(neg_f.reshape(bsz, 2 * n_grp, nk, 1, t))
        in_specs.append(pl.BlockSpec((1, 2, nk, 1, t), lambda b, g, i: (b, g, 0, 0, 0)))
    elif mode == "diff":
        bias, lq1, lk1, lq2, lk2, sub_g = extra
        args += [bias, lq1.reshape(1, -1), lk1.reshape(1, -1), lq2.reshape(1, -1), lk2.reshape(1, -1),
                 sub_g.reshape(1, -1)]
        in_specs.append(pl.BlockSpec((1, LANES, 2 * LANES), lambda b, g, i: (g, 0, 0)))
        in_specs += [_const_spec((1, DH_B))] * 4 + [_const_spec((1, 2 * DH_B))]
    scratch = [pltpu.VMEM((2, t, 1), F32), pltpu.VMEM((2, t, 2 * LANES), F32),
               pltpu.VMEM((2, t, t), F32), pltpu.VMEM((2, t, t), BF16), pltpu.VMEM((2, t, LANES), F32)]
    return pl.pallas_call(
        functools.partial(_attn_kernel, mode=mode, lam_init=lam_init),
        grid=(bsz, n_grp, s_len // t),
        in_specs=in_specs,
        out_specs=pl.BlockSpec((1, t, LANES), lambda b, g, i: (b, i, g)),
        out_shape=jax.ShapeDtypeStruct((bsz, s_len, n_grp * LANES), BF16),
        scratch_shapes=scratch,
        compiler_params=_cparams("parallel", "parallel", "arbitrary"),
        name="attn_" + mode,
    )(*args)


def _band_bias_kernel(e_ref, o_ref):
    rows, win = o_ref.shape[1], o_ref.shape[2]
    ext = jnp.broadcast_to(e_ref[0], (rows, e_ref.shape[-1]))
    o_ref[0] = pltpu.roll(ext, win + 1, 1, stride=1, stride_axis=0)[:, :win] * LOG2E


def _band_bias(rel_table):
    ext_len = BAND_WIN + BAND_GROUP
    flipped = rel_table[:, ::-1]
    left = BAND_LEAD - REL_CLIP + BAND_GROUP - 1
    ext = jnp.pad(flipped, ((0, 0), (left, ext_len - left - flipped.shape[1])), mode="edge")
    return pl.pallas_call(
        _band_bias_kernel,
        grid=(H_D,),
        in_specs=[pl.BlockSpec((1, 1, ext_len), lambda h: (h, 0, 0))],
        out_specs=pl.BlockSpec((1, BAND_GROUP, BAND_WIN), lambda h: (h, 0, 0)),
        out_shape=jax.ShapeDtypeStruct((H_D, BAND_GROUP, BAND_WIN), F32),
        compiler_params=_cparams("parallel"),
        name="band_bias",
    )(ext.reshape(H_D, 1, ext_len))


def _band_mask_kernel(o_ref):
    row = lax.broadcasted_iota(jnp.int32, o_ref.shape, 0) % BAND_GROUP
    col = lax.broadcasted_iota(jnp.int32, o_ref.shape, 1)
    first = (row // CHUNK) * CHUNK
    in_band = (col >= first) & (col < first + BAND_LEAD + CHUNK)
    o_ref[...] = jnp.where(in_band, col, -BAND_WIN * 2)


def _band_mask():
    shape = (2 * BAND_GROUP, BAND_WIN)
    return pl.pallas_call(
        _band_mask_kernel,
        out_specs=pl.BlockSpec(shape, lambda: (0, 0)),
        out_shape=jax.ShapeDtypeStruct(shape, jnp.int32),
        compiler_params=pltpu.CompilerParams(vmem_limit_bytes=VMEM_LIMIT),
        name="band_mask",
    )()


def _band_kernel(q_ref, kp_ref, kc_ref, vp_ref, vc_ref, bias_ref, vis_ref, o_ref):
    t = o_ref.shape[1]
    i = pl.program_id(2)
    lane = lax.broadcasted_iota(jnp.int32, (BAND_GROUP, LANES), 1)
    low = lane < LANES // 2
    kw = jnp.concatenate([kp_ref[0, 0], kc_ref[0, 0]], axis=0)
    vw = jnp.concatenate([vp_ref[0, 0], vc_ref[0, 0]], axis=0)
    for r in range(t // BAND_GROUP):
        q = q_ref[0, 0, r * BAND_GROUP:(r + 1) * BAND_GROUP]
        zero = jnp.zeros_like(q)
        q2 = jnp.concatenate([jnp.where(low, q, zero), jnp.where(low, zero, q)], axis=0)
        w0 = t - BAND_LEAD + r * BAND_GROUP
        keep = vis_ref[...] >= jnp.maximum(BAND_LEAD - r * BAND_GROUP - i * t, 0)
        sc = _dot_nt(q2, kw[w0:w0 + BAND_WIN]) + bias_ref[0]
        sc = jnp.where(keep, sc, NEG_INF)
        p = jnp.exp2((sc - jnp.max(sc, axis=1, keepdims=True)).astype(BF16))
        o2 = _dot(p, vw[w0:w0 + BAND_WIN])
        o2 = o2[:, :LANES] / o2[:, LANES:LANES + 1]
        o = jnp.where(low, o2[:BAND_GROUP], o2[BAND_GROUP:])
        o_ref[0, r * BAND_GROUP:(r + 1) * BAND_GROUP] = o.astype(o_ref.dtype)


def _band_attention(q, k, v, bias):
    bsz, n_grp, s_len, _ = q.shape
    t = min(BAND_TILE, s_len)
    assert t >= BAND_LEAD
    cur = lambda b, g, i: (b, g, i, 0)
    prev = lambda b, g, i: (b, g, jnp.maximum(i - 1, 0), 0)
    blk = (1, 1, t, LANES)
    vblk = (1, 1, t, 2 * LANES)
    return pl.pallas_call(
        _band_kernel,
        grid=(bsz, n_grp, s_len // t),
        in_specs=[pl.BlockSpec(blk, cur), pl.BlockSpec(blk, prev), pl.BlockSpec(blk, cur),
                  pl.BlockSpec(vblk, prev), pl.BlockSpec(vblk, cur),
                  pl.BlockSpec((1, 2 * BAND_GROUP, BAND_WIN), lambda b, g, i: (g, 0, 0)),
                  _const_spec((2 * BAND_GROUP, BAND_WIN))],
        out_specs=pl.BlockSpec((1, t, LANES), lambda b, g, i: (b, i, g)),
        out_shape=jax.ShapeDtypeStruct((bsz, s_len, n_grp * LANES), BF16),
        compiler_params=_cparams("parallel", "parallel", "parallel"),
        name="attn_band",
    )(q, k, k, v, v, bias.reshape(n_grp, 2 * BAND_GROUP, BAND_WIN), _band_mask())


def _out_ln_kernel(oa_ref, ob_ref, w_ref, x_ref, mod_ref, g_ref, b_ref, o_ref):
    d = x_ref.shape[-1]
    half = oa_ref.shape[-1]
    y = _dot(oa_ref[0], w_ref[:half]) + _dot(ob_ref[0], w_ref[half:])
    gate = mod_ref[0][:, 2 * d:]
    o_ref[0] = _layer_norm(DEEPNORM_ALPHA * x_ref[0] + gate * y, g_ref[...], b_ref[...])


def _out_ln(oa, ob, w_out, x, mod, ln_g, ln_b):
    bsz, s_len, d = x.shape
    tm = min(ROW_TILE, s_len)
    row = lambda b, t: (b, t, 0)
    return pl.pallas_call(
        _out_ln_kernel,
        grid=(bsz, s_len // tm),
        in_specs=[
            pl.BlockSpec((1, tm, oa.shape[-1]), row), pl.BlockSpec((1, tm, ob.shape[-1]), row),
            _const_spec(w_out.shape),
            pl.BlockSpec((1, tm, d), row),
            pl.BlockSpec((1, 1, 3 * d), lambda b, t: (b, 0, 0)),
            _const_spec((1, d)), _const_spec((1, d)),
        ],
        out_specs=pl.BlockSpec((1, tm, d), row),
        out_shape=jax.ShapeDtypeStruct((bsz, s_len, d), F32),
        compiler_params=_cparams("parallel", "parallel"),
        name="out_ln",
    )(oa, ob, w_out.astype(BF16), x, mod, ln_g.reshape(1, d), ln_b.reshape(1, d))


def _ffn_kernel(x_ref, mod_ref, wg_ref, wv_ref, cw_ref, cb_ref, wd_ref, g_ref, b_ref, o_ref, carry_ref):
    d = x_ref.shape[-1]
    tm = x_ref.shape[1]
    x = x_ref[0]
    u, gate = _modulate(x, mod_ref[0], d)
    u = u.astype(BF16)
    d_ff = wg_ref.shape[1]
    n_tiles = d_ff // MXU_COLS
    edges = [(c * n_tiles // FFN_COL_CHUNKS) * MXU_COLS for c in range(FFN_COL_CHUNKS)] + [d_ff]
    seq_start = pl.program_id(1) == 0
    y = None
    for c in range(FFN_COL_CHUNKS):
        cols = slice(edges[c], edges[c + 1])
        row = lax.broadcasted_iota(jnp.int32, (tm, edges[c + 1] - edges[c]), 0)
        g = _dot(u, wg_ref[:, cols])
        val = _dot(u, wv_ref[:, cols])
        prev = jnp.where(seq_start, 0.0, carry_ref[:, cols])
        g1 = jnp.where(row == 0, prev[7:8], pltpu.roll(g, 1, 0))
        g2 = jnp.where(row == 0, prev[6:7], jnp.where(row == 1, prev[7:8], pltpu.roll(g, 2, 0)))
        carry_ref[:, cols] = g[tm - 8:]
        cw = cw_ref[:, cols]
        gc = cw[0:1] * g2 + cw[1:2] * g1 + cw[2:3] * g + cb_ref[:, cols]
        hmid = (gc * jax.nn.sigmoid(gc) * val).astype(BF16)
        yc = _dot(hmid, wd_ref[cols, :])
        y = yc if y is None else y + yc
    o_ref[0] = _layer_norm(DEEPNORM_ALPHA * x + gate * y, g_ref[...], b_ref[...])


def _ffn(x, mod, w_gate, w_val, conv_w, conv_b, w_down, ln_g, ln_b):
    bsz, s_len, d = x.shape
    d_ff = w_gate.shape[1]
    tm = min(FFN_ROW_TILE, s_len)
    row = lambda b, t: (b, t, 0)
    once = lambda shape: pl.BlockSpec(shape, lambda b, t: (0,) * len(shape), pipeline_mode=pl.Buffered(1))
    return pl.pallas_call(
        _ffn_kernel,
        grid=(bsz, s_len // tm),
        in_specs=[
            pl.BlockSpec((1, tm, d), row),
            pl.BlockSpec((1, 1, 3 * d), lambda b, t: (b, 0, 0)),
            once((d, d_ff)), once((d, d_ff)),
            _const_spec((CONV_W, d_ff)), _const_spec((1, d_ff)),
            once((d_ff, d)),
            _const_spec((1, d)), _const_spec((1, d)),
        ],
        out_specs=pl.BlockSpec((1, tm, d), row),
        out_shape=jax.ShapeDtypeStruct((bsz, s_len, d), F32),
        scratch_shapes=[pltpu.VMEM((8, d_ff), F32)],
        compiler_params=_cparams("parallel", "arbitrary"),
        name="ffn",
    )(x, mod, w_gate.astype(BF16), w_val.astype(BF16), conv_w, conv_b.reshape(1, d_ff),
      w_down.astype(BF16), ln_g.reshape(1, d), ln_b.reshape(1, d))


def kernel(x, c, ada_w, ada_b, ln_g, ln_b, t5_table, ab_w_in, mla_q_norm, mla_w_uq, mla_kv_norm, mla_w_ukv, diff_lq1, diff_lk1, diff_lq2, diff_lk2, diff_sub_g, ab_w_out, cd_w_in, fox_b_f, chunk_rel_table, cd_w_out, ffn_w_gate, ffn_w_val, ffn_conv_w, ffn_conv_b, ffn_w_down):
    bsz, s_len, d = x.shape
    depth = ada_w.shape[0]
    mods = _adaln(c, ada_w, ada_b).reshape(depth, 2, bsz, 1, 3 * d)
    for i in range(depth):
        mod = mods[i, 0]
        if i % 2 == 0:
            e = i // 2
            qa, ka, va, qb, kb, vb = _proj_ab(x, mod, ab_w_in[e], mla_q_norm[e], mla_w_uq[e],
                                              mla_kv_norm[e], mla_w_ukv[e])
            o_first = _attention("mla", qa, ka, va)
            lam_init = 0.8 - 0.6 * math.exp(-0.3 * i)
            o_second = _attention("diff", qb, kb, vb,
                                  extra=(_t5_bias(t5_table), diff_lq1[e], diff_lk1[e], diff_lq2[e],
                                         diff_lk2[e], diff_sub_g[e]),
                                  lam_init=lam_init)
            w_out = ab_w_out[e]
        else:
            o = i // 2
            qc, kc, vc, qd, kd, vd, fl_t = _proj_cd(x, mod, cd_w_in[o])
            neg_f = _fox_scan(fl_t, fox_b_f[o])
            o_first = _attention("fox", qc, kc, vc, extra=(neg_f,))
            o_second = _band_attention(qd, kd, vd, _band_bias(chunk_rel_table[o]))
            w_out = cd_w_out[o]
        x = _out_ln(o_first, o_second, w_out, x, mod, ln_g[i, 0], ln_b[i, 0])
        x = _ffn(x, mods[i, 1], ffn_w_gate[i], ffn_w_val[i], ffn_conv_w[i], ffn_conv_b[i], ffn_w_down[i],
                 ln_g[i, 1], ln_b[i, 1])
    return x
```

```python
import functools
import math

import jax
import jax.numpy as jnp
from jax import lax
from jax.experimental import pallas as pl
from jax.experimental.pallas import tpu as pltpu

DEPTH = 2
CHUNK = 64
H_A, Q_LORA, KV_LORA, NOPE_DIM, ROPE_DIM, V_DIM_A = 8, 256, 128, 64, 32, 64
ROPE_BASE = 10000.0
H_B, DH_B, T5_BUCKETS, T5_MAX_DIST = 4, 64, 32, 128
H_C, DH_C = 8, 64
H_D, DH_D, BAND_CHUNKS, REL_CLIP = 8, 64, 8, 128
CONV_W = 3
EPS_LN = 1e-5
EPS_RMS = 1e-6
DEEPNORM_ALPHA = (2 * DEPTH) ** 0.25
NEG_INF = -1e30
LOG2E = math.log2(math.e)

MXU_COLS = 256
LANES = 128
ATTN_TILE = 1024
STRIP = 64
BAND_TILE = 512
ROW_TILE = 512
FFN_ROW_TILE = 512
FFN_COL_CHUNKS = 2
BAND_GROUP = 2 * CHUNK
BAND_LEAD = BAND_CHUNKS * CHUNK
BAND_WIN = BAND_LEAD + BAND_GROUP
VMEM_LIMIT = 56 * 1024 * 1024

BF16 = jnp.bfloat16
F32 = jnp.float32


def _cparams(*sem, flags=None):
    return pltpu.CompilerParams(dimension_semantics=sem, vmem_limit_bytes=VMEM_LIMIT, flags=flags)


def _dot(a, b):
    return jnp.dot(a, b, preferred_element_type=F32)


def _dot_nt(a, b):
    return lax.dot_general(a, b, (((1,), (1,)), ((), ())), preferred_element_type=F32)


def _layer_norm(z, g, b):
    mu = jnp.mean(z, axis=-1, keepdims=True)
    zc = z - mu
    var = jnp.mean(zc * zc, axis=-1, keepdims=True)
    return zc * lax.rsqrt(var + EPS_LN) * g + b


def _rms_norm(z, g):
    ms = jnp.mean(z * z, axis=-1, keepdims=True)
    return z * lax.rsqrt(ms + EPS_RMS) * g


def _ones_column(rows):
    lane = lax.broadcasted_iota(jnp.int32, (rows, LANES), 1)
    return jnp.where(lane == 0, 1.0, 0.0).astype(BF16)


def _const_spec(shape):
    return pl.BlockSpec(shape, lambda *_: (0,) * len(shape))


def _adaln_kernel(c_ref, w_ref, b_ref, o_ref):
    c = c_ref[...]
    cond = c * jax.nn.sigmoid(c)
    y = jnp.dot(cond, w_ref[0], preferred_element_type=F32, precision=lax.Precision.HIGHEST)
    one = jnp.where(pl.program_id(1) > 0, 1.0, 0.0).astype(F32)
    o_ref[0] = y + b_ref[0] + one


def _adaln(c, ada_w, ada_b):
    bsz, d = c.shape
    n_sub = ada_w.shape[0] * ada_w.shape[1]
    w = ada_w.reshape(n_sub, d, 3 * d)
    b = ada_b.reshape(n_sub, 1, 3 * d)
    return pl.pallas_call(
        _adaln_kernel,
        grid=(n_sub, 3),
        in_specs=[
            pl.BlockSpec((bsz, d), lambda l, n: (0, 0)),
            pl.BlockSpec((1, d, d), lambda l, n: (l, 0, n)),
            pl.BlockSpec((1, 1, d), lambda l, n: (l, 0, n)),
        ],
        out_specs=pl.BlockSpec((1, bsz, d), lambda l, n: (l, 0, n)),
        out_shape=jax.ShapeDtypeStruct((n_sub, bsz, 3 * d), F32),
        compiler_params=_cparams("arbitrary", "arbitrary"),
        name="adaln",
    )(c, w, b)


def _modulate(x, mod, d):
    return x * mod[:, d:2 * d] + mod[:, :d], mod[:, 2 * d:]


def _proj_ab_kernel(x_ref, mod_ref, w_in_ref, qn_ref, kvn_ref, wq_ref, wk_ref, wv_ref,
                    cq_ref, sq_ref, ck_ref, sk_ref,
                    qa_ref, ka_ref, va_ref, qb_ref, kb_ref, vb_ref):
    d = x_ref.shape[-1]
    u, _ = _modulate(x_ref[0], mod_ref[0], d)
    h = _dot(u.astype(BF16), w_in_ref[...])
    o = 0
    cq = h[:, o:o + Q_LORA]; o += Q_LORA
    ckv = h[:, o:o + KV_LORA]; o += KV_LORA
    kr_a = h[:, o:o + LANES]; o += LANES
    kr_b = h[:, o:o + LANES]; o += LANES
    wb = H_B * LANES
    qb = h[:, o:o + wb]; o += wb
    kb = h[:, o:o + wb]; o += wb
    vb = h[:, o:o + wb]

    nq = _rms_norm(cq, qn_ref[...]).astype(BF16)
    q12 = _dot(nq, wq_ref[...])
    cq_t, sq_t = cq_ref[...], sq_ref[...]
    for hh in range(H_A):
        a = q12[:, hh * LANES:(hh + 1) * LANES]
        b = q12[:, (H_A + hh) * LANES:(H_A + hh + 1) * LANES]
        qa_ref[0, hh] = (a * cq_t + b * sq_t).astype(BF16)

    nkv = _rms_norm(ckv, kvn_ref[...]).astype(BF16)
    kn = _dot(nkv, wk_ref[...])
    vv = _dot(nkv, wv_ref[...])
    k_rope = kr_a * ck_ref[...] + kr_b * sk_ref[...]
    for hh in range(H_A):
        ka_ref[0, hh] = (kn[:, hh * LANES:(hh + 1) * LANES] + k_rope).astype(BF16)
    ones_col = _ones_column(vv.shape[0])
    for p in range(H_A // 2):
        va_ref[0, p, :, :LANES] = vv[:, p * LANES:(p + 1) * LANES].astype(BF16)
        va_ref[0, p, :, LANES:] = ones_col
    scale_b = DH_B ** -0.5 * LOG2E
    for hh in range(H_B):
        sl = slice(hh * LANES, (hh + 1) * LANES)
        qb_ref[0, hh] = (qb[:, sl] * scale_b).astype(BF16)
        kb_ref[0, hh] = kb[:, sl].astype(BF16)
        vb_ref[0, hh, :, :LANES] = vb[:, sl].astype(BF16)
        vb_ref[0, hh, :, LANES:] = ones_col


def _rope_tables(s_len):
    half = ROPE_DIM // 2
    inv = jnp.power(ROPE_BASE, -jnp.arange(half, dtype=F32) / half)
    ang = jnp.arange(s_len, dtype=F32)[:, None] * inv[None, :]
    cos, sin = jnp.cos(ang), jnp.sin(ang)
    zeros_pad = jnp.zeros((s_len, LANES - NOPE_DIM - ROPE_DIM), F32)
    c_rope = jnp.concatenate([cos, cos], axis=1)
    s_rope = jnp.concatenate([-sin, sin], axis=1)
    cq = jnp.concatenate([jnp.ones((s_len, NOPE_DIM), F32), c_rope, zeros_pad], axis=1)
    ck = jnp.concatenate([jnp.zeros((s_len, NOPE_DIM), F32), c_rope, zeros_pad], axis=1)
    sk = jnp.concatenate([jnp.zeros((s_len, NOPE_DIM), F32), s_rope, zeros_pad], axis=1)
    scale_a = (NOPE_DIM + ROPE_DIM) ** -0.5 * LOG2E
    return cq * scale_a, sk * scale_a, ck, sk


def _swap_halves(w):
    half = w.shape[-1] // 2
    return jnp.concatenate([w[..., half:], w[..., :half]], axis=-1)


def _proj_ab_weights(w_in, w_uq, w_ukv):
    d = w_in.shape[0]
    wb = H_B * 2 * DH_B
    o = 0
    w_cq = w_in[:, o:o + Q_LORA]; o += Q_LORA
    w_ckv = w_in[:, o:o + KV_LORA]; o += KV_LORA
    w_kr = w_in[:, o:o + ROPE_DIM]; o += ROPE_DIM
    w_rest = w_in[:, o:o + 3 * wb]
    lead = jnp.zeros((d, NOPE_DIM), F32)
    tail = jnp.zeros((d, LANES - NOPE_DIM - ROPE_DIM), F32)
    w_kr_a = jnp.concatenate([lead, w_kr, tail], axis=1)
    w_kr_b = jnp.concatenate([lead, _swap_halves(w_kr), tail], axis=1)
    w_in_aug = jnp.concatenate([w_cq, w_ckv, w_kr_a, w_kr_b, w_rest], axis=1).astype(BF16)

    wq = w_uq.reshape(Q_LORA, H_A, NOPE_DIM + ROPE_DIM)
    wq_nope, wq_rope = wq[..., :NOPE_DIM], wq[..., NOPE_DIM:]
    zpad = jnp.zeros((Q_LORA, H_A, LANES - NOPE_DIM - ROPE_DIM), F32)
    wq1 = jnp.concatenate([wq_nope, wq_rope, zpad], axis=-1).reshape(Q_LORA, H_A * LANES)
    wq2 = jnp.concatenate([jnp.zeros_like(wq_nope), _swap_halves(wq_rope), zpad], axis=-1)
    wq12 = jnp.concatenate([wq1, wq2.reshape(Q_LORA, H_A * LANES)], axis=1).astype(BF16)

    wkv = w_ukv.reshape(KV_LORA, H_A, NOPE_DIM + V_DIM_A)
    wk = jnp.concatenate([wkv[..., :NOPE_DIM], jnp.zeros((KV_LORA, H_A, LANES - NOPE_DIM), F32)], axis=-1)
    wk = wk.reshape(KV_LORA, H_A * LANES).astype(BF16)
    wv = wkv[..., NOPE_DIM:].reshape(KV_LORA, H_A * V_DIM_A).astype(BF16)
    return w_in_aug, wq12, wk, wv


def _proj_ab(x, mod, w_in, q_norm, w_uq, kv_norm, w_ukv):
    bsz, s_len, d = x.shape
    tm = min(ROW_TILE, s_len)
    w_in_aug, wq12, wk, wv = _proj_ab_weights(w_in, w_uq, w_ukv)
    tabs = _rope_tables(s_len)
    row = lambda b, t: (b, t, 0)
    head = lambda b, t: (b, 0, t, 0)
    tab_spec = pl.BlockSpec((tm, LANES), lambda b, t: (t, 0))
    out = lambda h, w=LANES: jax.ShapeDtypeStruct((bsz, h, s_len, w), BF16)
    out_spec = lambda h, w=LANES: pl.BlockSpec((1, h, tm, w), head)
    return pl.pallas_call(
        _proj_ab_kernel,
        grid=(bsz, s_len // tm),
        in_specs=[
            pl.BlockSpec((1, tm, d), row),
            pl.BlockSpec((1, 1, 3 * d), lambda b, t: (b, 0, 0)),
            _const_spec(w_in_aug.shape),
            _const_spec((1, Q_LORA)), _const_spec((1, KV_LORA)),
            _const_spec(wq12.shape), _const_spec(wk.shape), _const_spec(wv.shape),
            tab_spec, tab_spec, tab_spec, tab_spec,
        ],
        out_specs=[out_spec(H_A), out_spec(H_A), out_spec(H_A // 2, 2 * LANES),
                   out_spec(H_B), out_spec(H_B), out_spec(H_B, 2 * LANES)],
        out_shape=[out(H_A), out(H_A), out(H_A // 2, 2 * LANES), out(H_B), out(H_B), out(H_B, 2 * LANES)],
        compiler_params=_cparams("parallel", "parallel"),
        name="proj_ab",
    )(x, mod, w_in_aug, q_norm.reshape(1, -1), kv_norm.reshape(1, -1), wq12, wk, wv, *tabs)


def _proj_cd_kernel(x_ref, mod_ref, w_ref, wf_ref, qc_ref, kc_ref, vc_ref, qd_ref, kd_ref, vd_ref, fl_ref):
    d = x_ref.shape[-1]
    u, _ = _modulate(x_ref[0], mod_ref[0], d)
    u = u.astype(BF16)
    h = _dot(u, w_ref[...])
    fl_ref[0] = _dot_nt(wf_ref[...], u)
    scale = DH_C ** -0.5 * LOG2E
    n_pair = H_C // 2
    for k, (ref, sc) in enumerate(((qc_ref, scale), (kc_ref, None), (vc_ref, None),
                                   (qd_ref, scale), (kd_ref, None), (vd_ref, None))):
        for p in range(n_pair):
            blk = h[:, (k * n_pair + p) * LANES:(k * n_pair + p + 1) * LANES]
            if sc is not None:
                blk = blk * sc
            ref[0, p, :, :LANES] = blk.astype(BF16)
            if ref is vc_ref or ref is vd_ref:
                ref[0, p, :, LANES:] = _ones_column(blk.shape[0])


def _proj_cd(x, mod, w_in):
    bsz, s_len, d = x.shape
    tm = min(ROW_TILE, s_len)
    wc, wd = H_C * DH_C, H_D * DH_D
    o = 3 * wc
    w_main = jnp.concatenate([w_in[:, :o], w_in[:, o + H_C:]], axis=1).astype(BF16)
    w_f = w_in[:, o:o + H_C].T.astype(BF16)
    n_pair = H_C // 2
    head = lambda b, t: (b, 0, t, 0)
    out = lambda w: jax.ShapeDtypeStruct((bsz, n_pair, s_len, w), BF16)
    out_spec = lambda w: pl.BlockSpec((1, n_pair, tm, w), head)
    widths = (LANES, LANES, 2 * LANES, LANES, LANES, 2 * LANES)
    return pl.pallas_call(
        _proj_cd_kernel,
        grid=(bsz, s_len // tm),
        in_specs=[
            pl.BlockSpec((1, tm, d), lambda b, t: (b, t, 0)),
            pl.BlockSpec((1, 1, 3 * d), lambda b, t: (b, 0, 0)),
            _const_spec(w_main.shape), _const_spec(w_f.shape),
        ],
        out_specs=[out_spec(w) for w in widths] + [pl.BlockSpec((1, H_C, tm), lambda b, t: (b, 0, t))],
        out_shape=[out(w) for w in widths] + [jax.ShapeDtypeStruct((bsz, H_C, s_len), F32)],
        compiler_params=_cparams("parallel", "parallel"),
        name="proj_cd",
    )(x, mod, w_main, w_f)


def _fox_scan_kernel(fl_ref, bf_ref, o_ref):
    z = fl_ref[0] + bf_ref[...]
    x = jnp.minimum(z, 0.0) - jnp.log1p(jnp.exp(-jnp.abs(z)))
    s_len = x.shape[-1]
    pos = lax.broadcasted_iota(jnp.int32, x.shape, 1)
    sh = 1
    while sh < s_len:
        x = x + jnp.where(pos >= sh, pltpu.roll(x, sh, 1), 0.0)
        sh *= 2
    o_ref[0] = -x * LOG2E


def _fox_scan(fl_t, b_f):
    bsz, h, s_len = fl_t.shape
    return pl.pallas_call(
        _fox_scan_kernel,
        grid=(bsz,),
        in_specs=[pl.BlockSpec((1, h, s_len), lambda b: (b, 0, 0)), _const_spec((h, 1))],
        out_specs=pl.BlockSpec((1, h, s_len), lambda b: (b, 0, 0)),
        out_shape=jax.ShapeDtypeStruct((bsz, h, s_len), F32),
        compiler_params=_cparams("parallel"),
        name="fox_scan",
    )(fl_t, b_f.reshape(h, 1))


def _t5_bias_kernel(tab_ref, o_ref):
    h = pl.program_id(0)
    half = T5_BUCKETS // 2
    max_exact = half // 2
    shape = (LANES, 2 * LANES)
    qq = lax.broadcasted_iota(jnp.int32, shape, 0)
    kk = lax.broadcasted_iota(jnp.int32, shape, 1)
    rel = kk - LANES - qq
    n = jnp.abs(rel)
    large = max_exact + (jnp.log(jnp.maximum(n, 1).astype(F32) / max_exact)
                         / math.log(T5_MAX_DIST / max_exact) * (half - max_exact)).astype(jnp.int32)
    large = jnp.minimum(large, half - 1)
    bucket = jnp.where(rel > 0, half, 0) + jnp.where(n < max_exact, n, large)
    val = jnp.zeros(shape, F32)
    for b in range(T5_BUCKETS):
        val = jnp.where(bucket == b, tab_ref[h, b], val)
    o_ref[0] = (val - tab_ref[h, half - 1]) * LOG2E


def _t5_bias(t5_table):
    return pl.pallas_call(
        _t5_bias_kernel,
        grid=(H_B,),
        in_specs=[pl.BlockSpec(memory_space=pltpu.SMEM)],
        out_specs=pl.BlockSpec((1, LANES, 2 * LANES), lambda h: (h, 0, 0)),
        out_shape=jax.ShapeDtypeStruct((H_B, LANES, 2 * LANES), F32),
        compiler_params=_cparams("parallel"),
        name="t5_bias",
    )(t5_table)


def _attn_kernel(*refs, mode, lam_init):
    if mode == "mla":
        q_ref, k_ref, v_ref, o_ref, m_ref, acc_ref, s_ref, p_ref, mb_ref = refs
    elif mode == "fox":
        q_ref, k_ref, v_ref, nf_ref, o_ref, m_ref, acc_ref, s_ref, p_ref, mb_ref = refs
    else:
        (q_ref, k_ref, v_ref, bias_ref, lq1_ref, lk1_ref, lq2_ref, lk2_ref, subg_ref,
         o_ref, m_ref, acc_ref, s_ref, p_ref, mb_ref) = refs
    t = o_ref.shape[1]
    i = pl.program_id(2)
    lane = lax.broadcasted_iota(jnp.int32, (t, LANES), 1)
    low = lane < LANES // 2
    if mode == "mla":
        qs = (q_ref[0, 0], q_ref[0, 1])
    else:
        q = q_ref[0, 0]
        zero = jnp.zeros_like(q)
        qs = (jnp.where(low, q, zero), jnp.where(low, zero, q))

    m_ref[...] = jnp.full(m_ref.shape, NEG_INF, F32)
    acc_ref[...] = jnp.zeros(acc_ref.shape, F32)

    def logits(s, j):
        k = k_ref[0, s, j] if mode == "mla" else k_ref[0, 0, j]
        sc = _dot_nt(qs[s], k)
        if mode == "fox":
            sc = sc + nf_ref[0, s, j]
        s_ref[s] = sc

    def softmax_pv(s, j, diag):
        if mode == "diff":
            if diag:
                s_ref[s, :LANES, :LANES] += bias_ref[0, :, LANES:]
                for r in range(1, t // LANES):
                    s_ref[s, r * LANES:(r + 1) * LANES, (r - 1) * LANES:(r + 1) * LANES] += bias_ref[0]
            else:
                is_prev = (j == i - 1).astype(F32)
                s_ref[s, :LANES, t - LANES:] += bias_ref[0, :, :LANES] * is_prev

        def visible_cols(r):
            if not diag:
                return t
            last_row = (r + 1) * STRIP - 1
            last_col = last_row if mode == "fox" else (last_row // CHUNK + 1) * CHUNK - 1
            return (last_col // LANES + 1) * LANES

        m_all = m_ref[s]
        m_parts, a_parts = [], []
        for r in range(t // STRIP):
            rows = slice(r * STRIP, (r + 1) * STRIP)
            ncol = visible_cols(r)
            if diag:
                row = lax.broadcasted_iota(jnp.int32, (STRIP, LANES), 0) + r * STRIP
                col = lax.broadcasted_iota(jnp.int32, (STRIP, LANES), 1) + (ncol - LANES)
                keep = (col <= row) if mode == "fox" else (col // CHUNK) <= (row // CHUNK)
                edge = jnp.where(keep, s_ref[s, rows, ncol - LANES:ncol], NEG_INF)
                s_ref[s, rows, ncol - LANES:ncol] = edge
            blocks = [s_ref[s, rows, c * LANES:(c + 1) * LANES] for c in range(ncol // LANES)]
            mx = functools.reduce(jnp.maximum, blocks)
            m_prev = m_all[rows]
            m_new = jnp.maximum(m_prev, jnp.max(mx, axis=1, keepdims=True))
            m_parts.append(m_new)
            a_parts.append(jnp.exp2(m_prev - m_new))
            mb_ref[s, rows] = jnp.broadcast_to(m_new, (STRIP, LANES))
        for r in range(t // STRIP):
            rows = slice(r * STRIP, (r + 1) * STRIP)
            ncol = visible_cols(r)
            mb = mb_ref[s, rows]
            for c in range(ncol // LANES):
                cols = slice(c * LANES, (c + 1) * LANES)
                p_ref[s, rows, cols] = jnp.exp2((s_ref[s, rows, cols] - mb).astype(BF16))
            if ncol < t:
                p_ref[s, rows, ncol:] = jnp.zeros((STRIP, t - ncol), BF16)
        m_ref[s] = jnp.concatenate(m_parts, axis=0)
        acc_ref[s] = jnp.concatenate(a_parts, axis=0) * acc_ref[s] + _dot(p_ref[s], v_ref[0, 0, j])

    logits(0, 0)

    def far_tile(j):
        logits(1, j)
        softmax_pv(0, j, False)
        logits(0, j + 1)
        softmax_pv(1, j, False)

    odd = i % 2

    @pl.when(odd == 1)
    def _():
        far_tile(0)

    def far_pair(jj, carry):
        far_tile(odd + 2 * jj)
        far_tile(odd + 2 * jj + 1)
        return carry

    lax.fori_loop(0, i // 2, far_pair, 0)
    logits(1, i)
    softmax_pv(0, i, True)
    softmax_pv(1, i, True)

    o0 = acc_ref[0, :, :LANES] / acc_ref[0, :, LANES:LANES + 1]
    o1 = acc_ref[1, :, :LANES] / acc_ref[1, :, LANES:LANES + 1]
    if mode == "diff":
        lam = (jnp.exp(jnp.sum(lq1_ref[...] * lk1_ref[...])) - jnp.exp(jnp.sum(lq2_ref[...] * lk2_ref[...]))
               + lam_init)
        o = _rms_norm(o0 - lam * o1, subg_ref[...]) * (1.0 - lam_init)
    else:
        o = jnp.where(low, o0, o1)
    o_ref[0] = o.astype(o_ref.dtype)


def _attention(mode, q, k, v, extra=(), lam_init=0.0):
    bsz, _, s_len, _ = q.shape
    t = min(ATTN_TILE, s_len)
    nk = s_len // t
    n_grp = v.shape[1]
    per = 2 if mode == "mla" else 1
    k5 = k.reshape(bsz, k.shape[1], nk, t, LANES)
    v5 = v.reshape(bsz, n_grp, nk, t, 2 * LANES)
    in_specs = [
        pl.BlockSpec((1, per, t, LANES), lambda b, g, i: (b, g, i, 0)),
        pl.BlockSpec((1, per, nk, t, LANES), lambda b, g, i: (b, g, 0, 0, 0)),
        pl.BlockSpec((1, 1, nk, t, 2 * LANES), lambda b, g, i: (b, g, 0, 0, 0)),
    ]
    args = [q, k5, v5]
    if mode == "fox":
        (neg_f,) = extra
        args.append(neg_f.reshape(bsz, 2 * n_grp, nk, 1, t))
        in_specs.append(pl.BlockSpec((1, 2, nk, 1, t), lambda b, g, i: (b, g, 0, 0, 0)))
    elif mode == "diff":
        bias, lq1, lk1, lq2, lk2, sub_g = extra
        args += [bias, lq1.reshape(1, -1), lk1.reshape(1, -1), lq2.reshape(1, -1), lk2.reshape(1, -1),
                 sub_g.reshape(1, -1)]
        in_specs.append(pl.BlockSpec((1, LANES, 2 * LANES), lambda b, g, i: (g, 0, 0)))
        in_specs += [_const_spec((1, DH_B))] * 4 + [_const_spec((1, 2 * DH_B))]
    scratch = [pltpu.VMEM((2, t, 1), F32), pltpu.VMEM((2, t, 2 * LANES), F32),
               pltpu.VMEM((2, t, t), F32), pltpu.VMEM((2, t, t), BF16), pltpu.VMEM((2, t, LANES), F32)]
    return pl.pallas_call(
        functools.partial(_attn_kernel, mode=mode, lam_init=lam_init),
        grid=(bsz, n_grp, s_len // t),
        in_specs=in_specs,
        out_specs=pl.BlockSpec((1, t, LANES), lambda b, g, i: (b, i, g)),
        out_shape=jax.ShapeDtypeStruct((bsz, s_len, n_grp * LANES), BF16),
        scratch_shapes=scratch,
        compiler_params=_cparams("parallel", "parallel", "arbitrary"),
        name="attn_" + mode,
    )(*args)


def _band_bias_kernel(e_ref, o_ref):
    rows, win = o_ref.shape[1], o_ref.shape[2]
    ext = jnp.broadcast_to(e_ref[0], (rows, e_ref.shape[-1]))
    o_ref[0] = pltpu.roll(ext, win + 1, 1, stride=1, stride_axis=0)[:, :win] * LOG2E


def _band_bias(rel_table):
    ext_len = BAND_WIN + BAND_GROUP
    flipped = rel_table[:, ::-1]
    left = BAND_LEAD - REL_CLIP + BAND_GROUP - 1
    ext = jnp.pad(flipped, ((0, 0), (left, ext_len - left - flipped.shape[1])), mode="edge")
    return pl.pallas_call(
        _band_bias_kernel,
        grid=(H_D,),
        in_specs=[pl.BlockSpec((1, 1, ext_len), lambda h: (h, 0, 0))],
        out_specs=pl.BlockSpec((1, BAND_GROUP, BAND_WIN), lambda h: (h, 0, 0)),
        out_shape=jax.ShapeDtypeStruct((H_D, BAND_GROUP, BAND_WIN), F32),
        compiler_params=_cparams("parallel"),
        name="band_bias",
    )(ext.reshape(H_D, 1, ext_len))


def _band_mask_kernel(o_ref):
    row = lax.broadcasted_iota(jnp.int32, o_ref.shape, 0) % BAND_GROUP
    col = lax.broadcasted_iota(jnp.int32, o_ref.shape, 1)
    first = (row // CHUNK) * CHUNK
    in_band = (col >= first) & (col < first + BAND_LEAD + CHUNK)
    o_ref[...] = jnp.where(in_band, col, -BAND_WIN * 2)


def _band_mask():
    shape = (2 * BAND_GROUP, BAND_WIN)
    return pl.pallas_call(
        _band_mask_kernel,
        out_specs=pl.BlockSpec(shape, lambda: (0, 0)),
        out_shape=jax.ShapeDtypeStruct(shape, jnp.int32),
        compiler_params=pltpu.CompilerParams(vmem_limit_bytes=VMEM_LIMIT),
        name="band_mask",
    )()


def _band_kernel(q_ref, kp_ref, kc_ref, vp_ref, vc_ref, bias_ref, vis_ref, o_ref):
    t = o_ref.shape[1]
    i = pl.program_id(2)
    lane = lax.broadcasted_iota(jnp.int32, (BAND_GROUP, LANES), 1)
    low = lane < LANES // 2
    kw = jnp.concatenate([kp_ref[0, 0], kc_ref[0, 0]], axis=0)
    vw = jnp.concatenate([vp_ref[0, 0], vc_ref[0, 0]], axis=0)
    for r in range(t // BAND_GROUP):
        q = q_ref[0, 0, r * BAND_GROUP:(r + 1) * BAND_GROUP]
        zero = jnp.zeros_like(q)
        q2 = jnp.concatenate([jnp.where(low, q, zero), jnp.where(low, zero, q)], axis=0)
        w0 = t - BAND_LEAD + r * BAND_GROUP
        keep = vis_ref[...] >= jnp.maximum(BAND_LEAD - r * BAND_GROUP - i * t, 0)
        sc = _dot_nt(q2, kw[w0:w0 + BAND_WIN]) + bias_ref[0]
        sc = jnp.where(keep, sc, NEG_INF)
        p = jnp.exp2((sc - jnp.max(sc, axis=1, keepdims=True)).astype(BF16))
        o2 = _dot(p, vw[w0:w0 + BAND_WIN])
        o2 = o2[:, :LANES] / o2[:, LANES:LANES + 1]
        o = jnp.where(low, o2[:BAND_GROUP], o2[BAND_GROUP:])
        o_ref[0, r * BAND_GROUP:(r + 1) * BAND_GROUP] = o.astype(o_ref.dtype)


def _band_attention(q, k, v, bias):
    bsz, n_grp, s_len, _ = q.shape
    t = min(BAND_TILE, s_len)
    assert t >= BAND_LEAD
    cur = lambda b, g, i: (b, g, i, 0)
    prev = lambda b, g, i: (b, g, jnp.maximum(i - 1, 0), 0)
    blk = (1, 1, t, LANES)
    vblk = (1, 1, t, 2 * LANES)
    return pl.pallas_call(
        _band_kernel,
        grid=(bsz, n_grp, s_len // t),
        in_specs=[pl.BlockSpec(blk, cur), pl.BlockSpec(blk, prev), pl.BlockSpec(blk, cur),
                  pl.BlockSpec(vblk, prev), pl.BlockSpec(vblk, cur),
                  pl.BlockSpec((1, 2 * BAND_GROUP, BAND_WIN), lambda b, g, i: (g, 0, 0)),
                  _const_spec((2 * BAND_GROUP, BAND_WIN))],
        out_specs=pl.BlockSpec((1, t, LANES), lambda b, g, i: (b, i, g)),
        out_shape=jax.ShapeDtypeStruct((bsz, s_len, n_grp * LANES), BF16),
        compiler_params=_cparams("parallel", "parallel", "parallel"),
        name="attn_band",
    )(q, k, k, v, v, bias.reshape(n_grp, 2 * BAND_GROUP, BAND_WIN), _band_mask())


def _out_ln_kernel(oa_ref, ob_ref, w_ref, x_ref, mod_ref, g_ref, b_ref, o_ref):
    d = x_ref.shape[-1]
    half = oa_ref.shape[-1]
    y = _dot(oa_ref[0], w_ref[:half]) + _dot(ob_ref[0], w_ref[half:])
    gate = mod_ref[0][:, 2 * d:]
    o_ref[0] = _layer_norm(DEEPNORM_ALPHA * x_ref[0] + gate * y, g_ref[...], b_ref[...])


def _out_ln(oa, ob, w_out, x, mod, ln_g, ln_b):
    bsz, s_len, d = x.shape
    tm = min(ROW_TILE, s_len)
    row = lambda b, t: (b, t, 0)
    return pl.pallas_call(
        _out_ln_kernel,
        grid=(bsz, s_len // tm),
        in_specs=[
            pl.BlockSpec((1, tm, oa.shape[-1]), row), pl.BlockSpec((1, tm, ob.shape[-1]), row),
            _const_spec(w_out.shape),
            pl.BlockSpec((1, tm, d), row),
            pl.BlockSpec((1, 1, 3 * d), lambda b, t: (b, 0, 0)),
            _const_spec((1, d)), _const_spec((1, d)),
        ],
        out_specs=pl.BlockSpec((1, tm, d), row),
        out_shape=jax.ShapeDtypeStruct((bsz, s_len, d), F32),
        compiler_params=_cparams("parallel", "parallel"),
        name="out_ln",
    )(oa, ob, w_out.astype(BF16), x, mod, ln_g.reshape(1, d), ln_b.reshape(1, d))


def _ffn_kernel(x_ref, mod_ref, wg_ref, wv_ref, cw_ref, cb_ref, wd_ref, g_ref, b_ref, o_ref, carry_ref):
    d = x_ref.shape[-1]
    tm = x_ref.shape[1]
    x = x_ref[0]
    u, gate = _modulate(x, mod_ref[0], d)
    u = u.astype(BF16)
    d_ff = wg_ref.shape[1]
    n_tiles = d_ff // MXU_COLS
    edges = [(c * n_tiles // FFN_COL_CHUNKS) * MXU_COLS for c in range(FFN_COL_CHUNKS)] + [d_ff]
    seq_start = pl.program_id(1) == 0
    y = None
    for c in range(FFN_COL_CHUNKS):
        cols = slice(edges[c], edges[c + 1])
        row = lax.broadcasted_iota(jnp.int32, (tm, edges[c + 1] - edges[c]), 0)
        g = _dot(u, wg_ref[:, cols])
        val = _dot(u, wv_ref[:, cols])
        prev = jnp.where(seq_start, 0.0, carry_ref[:, cols])
        g1 = jnp.where(row == 0, prev[7:8], pltpu.roll(g, 1, 0))
        g2 = jnp.where(row == 0, prev[6:7], jnp.where(row == 1, prev[7:8], pltpu.roll(g, 2, 0)))
        carry_ref[:, cols] = g[tm - 8:]
        cw = cw_ref[:, cols]
        gc = cw[0:1] * g2 + cw[1:2] * g1 + cw[2:3] * g + cb_ref[:, cols]
        hmid = (gc * jax.nn.sigmoid(gc) * val).astype(BF16)
        yc = _dot(hmid, wd_ref[cols, :])
        y = yc if y is None else y + yc
    o_ref[0] = _layer_norm(DEEPNORM_ALPHA * x + gate * y, g_ref[...], b_ref[...])


def _ffn(x, mod, w_gate, w_val, conv_w, conv_b, w_down, ln_g, ln_b):
    bsz, s_len, d = x.shape
    d_ff = w_gate.shape[1]
    tm = min(FFN_ROW_TILE, s_len)
    row = lambda b, t: (b, t, 0)
    once = lambda shape: pl.BlockSpec(shape, lambda b, t: (0,) * len(shape), pipeline_mode=pl.Buffered(1))
    return pl.pallas_call(
        _ffn_kernel,
        grid=(bsz, s_len // tm),
        in_specs=[
            pl.BlockSpec((1, tm, d), row),
            pl.BlockSpec((1, 1, 3 * d), lambda b, t: (b, 0, 0)),
            once((d, d_ff)), once((d, d_ff)),
            _const_spec((CONV_W, d_ff)), _const_spec((1, d_ff)),
            once((d_ff, d)),
            _const_spec((1, d)), _const_spec((1, d)),
        ],
        out_specs=pl.BlockSpec((1, tm, d), row),
        out_shape=jax.ShapeDtypeStruct((bsz, s_len, d), F32),
        scratch_shapes=[pltpu.VMEM((8, d_ff), F32)],
        compiler_params=_cparams("parallel", "arbitrary"),
        name="ffn",
    )(x, mod, w_gate.astype(BF16), w_val.astype(BF16), conv_w, conv_b.reshape(1, d_ff),
      w_down.astype(BF16), ln_g.reshape(1, d), ln_b.reshape(1, d))


def kernel(x, c, ada_w, ada_b, ln_g, ln_b, t5_table, ab_w_in, mla_q_norm, mla_w_uq, mla_kv_norm, mla_w_ukv, diff_lq1, diff_lk1, diff_lq2, diff_lk2, diff_sub_g, ab_w_out, cd_w_in, fox_b_f, chunk_rel_table, cd_w_out, ffn_w_gate, ffn_w_val, ffn_conv_w, ffn_conv_b, ffn_w_down):
    bsz, s_len, d = x.shape
    depth = ada_w.shape[0]
    mods = _adaln(c, ada_w, ada_b).reshape(depth, 2, bsz, 1, 3 * d)
    for i in range(depth):
        mod = mods[i, 0]
        if i % 2 == 0:
            e = i // 2
            qa, ka, va, qb, kb, vb = _proj_ab(x, mod, ab_w_in[e], mla_q_norm[e], mla_w_uq[e],
                                              mla_kv_norm[e], mla_w_ukv[e])
            o_first = _attention("mla", qa, ka, va)
            lam_init = 0.8 - 0.6 * math.exp(-0.3 * i)
            o_second = _attention("diff", qb, kb, vb,
                                  extra=(_t5_bias(t5_table), diff_lq1[e], diff_lk1[e], diff_lq2[e],
                                         diff_lk2[e], diff_sub_g[e]),
                                  lam_init=lam_init)
            w_out = ab_w_out[e]
        else:
            o = i // 2
            qc, kc, vc, qd, kd, vd, fl_t = _proj_cd(x, mod, cd_w_in[o])
            neg_f = _fox_scan(fl_t, fox_b_f[o])
            o_first = _attention("fox", qc, kc, vc, extra=(neg_f,))
            o_second = _band_attention(qd, kd, vd, _band_bias(chunk_rel_table[o]))
            w_out = cd_w_out[o]
        x = _out_ln(o_first, o_second, w_out, x, mod, ln_g[i, 0], ln_b[i, 0])
        x = _ffn(x, mods[i, 1], ffn_w_gate[i], ffn_w_val[i], ffn_conv_w[i], ffn_conv_b[i], ffn_w_down[i],
                 ln_g[i, 1], ln_b[i, 1])
    return x
```

```python
import functools
import math

import jax
import jax.numpy as jnp
from jax import lax
from jax.experimental import pallas as pl
from jax.experimental.pallas import tpu as pltpu

DEPTH = 2
CHUNK = 64
H_A, Q_LORA, KV_LORA, NOPE_DIM, ROPE_DIM, V_DIM_A = 8, 256, 128, 64, 32, 64
ROPE_BASE = 10000.0
H_B, DH_B, T5_BUCKETS, T5_MAX_DIST = 4, 64, 32, 128
H_C, DH_C = 8, 64
H_D, DH_D, BAND_CHUNKS, REL_CLIP = 8, 64, 8, 128
CONV_W = 3
EPS_LN = 1e-5
EPS_RMS = 1e-6
DEEPNORM_ALPHA = (2 * DEPTH) ** 0.25
NEG_INF = -1e30
LOG2E = math.log2(math.e)

MXU_COLS = 256
LANES = 128
ATTN_TILE = 1024
STRIP = 64
BAND_TILE = 512
ROW_TILE = 512
FFN_ROW_TILE = 512
FFN_COL_CHUNKS = 2
BAND_GROUP = 2 * CHUNK
BAND_LEAD = BAND_CHUNKS * CHUNK
BAND_WIN = BAND_LEAD + BAND_GROUP
VMEM_LIMIT = 56 * 1024 * 1024

BF16 = jnp.bfloat16
F32 = jnp.float32


def _cparams(*sem, flags=None):
    return pltpu.CompilerParams(dimension_semantics=sem, vmem_limit_bytes=VMEM_LIMIT, flags=flags)


def _dot(a, b):
    return jnp.dot(a, b, preferred_element_type=F32)


def _dot_nt(a, b):
    return lax.dot_general(a, b, (((1,), (1,)), ((), ())), preferred_element_type=F32)


def _layer_norm(z, g, b):
    mu = jnp.mean(z, axis=-1, keepdims=True)
    zc = z - mu
    var = jnp.mean(zc * zc, axis=-1, keepdims=True)
    return zc * lax.rsqrt(var + EPS_LN) * g + b


def _rms_norm(z, g):
    ms = jnp.mean(z * z, axis=-1, keepdims=True)
    return z * lax.rsqrt(ms + EPS_RMS) * g


def _ones_column(rows):
    lane = lax.broadcasted_iota(jnp.int32, (rows, LANES), 1)
    return jnp.where(lane == 0, 1.0, 0.0).astype(BF16)


def _const_spec(shape):
    return pl.BlockSpec(shape, lambda *_: (0,) * len(shape))


def _adaln_kernel(c_ref, w_ref, b_ref, o_ref):
    c = c_ref[...]
    cond = c * jax.nn.sigmoid(c)
    y = jnp.dot(cond, w_ref[0], preferred_element_type=F32, precision=lax.Precision.HIGHEST)
    one = jnp.where(pl.program_id(1) > 0, 1.0, 0.0).astype(F32)
    o_ref[0] = y + b_ref[0] + one


def _adaln(c, ada_w, ada_b):
    bsz, d = c.shape
    n_sub = ada_w.shape[0] * ada_w.shape[1]
    w = ada_w.reshape(n_sub, d, 3 * d)
    b = ada_b.reshape(n_sub, 1, 3 * d)
    return pl.pallas_call(
        _adaln_kernel,
        grid=(n_sub, 3),
        in_specs=[
            pl.BlockSpec((bsz, d), lambda l, n: (0, 0)),
            pl.BlockSpec((1, d, d), lambda l, n: (l, 0, n)),
            pl.BlockSpec((1, 1, d), lambda l, n: (l, 0, n)),
        ],
        out_specs=pl.BlockSpec((1, bsz, d), lambda l, n: (l, 0, n)),
        out_shape=jax.ShapeDtypeStruct((n_sub, bsz, 3 * d), F32),
        compiler_params=_cparams("arbitrary", "arbitrary"),
        name="adaln",
    )(c, w, b)


def _modulate(x, mod, d):
    return x * mod[:, d:2 * d] + mod[:, :d], mod[:, 2 * d:]


def _proj_ab_kernel(x_ref, mod_ref, w_in_ref, qn_ref, kvn_ref, wq_ref, wk_ref, wv_ref,
                    cq_ref, sq_ref, ck_ref, sk_ref,
                    qa_ref, ka_ref, va_ref, qb_ref, kb_ref, vb_ref):
    d = x_ref.shape[-1]
    u, _ = _modulate(x_ref[0], mod_ref[0], d)
    h = _dot(u.astype(BF16), w_in_ref[...])
    o = 0
    cq = h[:, o:o + Q_LORA]; o += Q_LORA
    ckv = h[:, o:o + KV_LORA]; o += KV_LORA
    kr_a = h[:, o:o + LANES]; o += LANES
    kr_b = h[:, o:o + LANES]; o += LANES
    wb = H_B * LANES
    qb = h[:, o:o + wb]; o += wb
    kb = h[:, o:o + wb]; o += wb
    vb = h[:, o:o + wb]

    nq = _rms_norm(cq, qn_ref[...]).astype(BF16)
    q12 = _dot(nq, wq_ref[...])
    cq_t, sq_t = cq_ref[...], sq_ref[...]
    for hh in range(H_A):
        a = q12[:, hh * LANES:(hh + 1) * LANES]
        b = q12[:, (H_A + hh) * LANES:(H_A + hh + 1) * LANES]
        qa_ref[0, hh] = (a * cq_t + b * sq_t).astype(BF16)

    nkv = _rms_norm(ckv, kvn_ref[...]).astype(BF16)
    kn = _dot(nkv, wk_ref[...])
    vv = _dot(nkv, wv_ref[...])
    k_rope = kr_a * ck_ref[...] + kr_b * sk_ref[...]
    for hh in range(H_A):
        ka_ref[0, hh] = (kn[:, hh * LANES:(hh + 1) * LANES] + k_rope).astype(BF16)
    ones_col = _ones_column(vv.shape[0])
    for p in range(H_A // 2):
        va_ref[0, p, :, :LANES] = vv[:, p * LANES:(p + 1) * LANES].astype(BF16)
        va_ref[0, p, :, LANES:] = ones_col
    scale_b = DH_B ** -0.5 * LOG2E
    for hh in range(H_B):
        sl = slice(hh * LANES, (hh + 1) * LANES)
        qb_ref[0, hh] = (qb[:, sl] * scale_b).astype(BF16)
        kb_ref[0, hh] = kb[:, sl].astype(BF16)
        vb_ref[0, hh, :, :LANES] = vb[:, sl].astype(BF16)
        vb_ref[0, hh, :, LANES:] = ones_col


def _rope_tables(s_len):
    half = ROPE_DIM // 2
    inv = jnp.power(ROPE_BASE, -jnp.arange(half, dtype=F32) / half)
    ang = jnp.arange(s_len, dtype=F32)[:, None] * inv[None, :]
    cos, sin = jnp.cos(ang), jnp.sin(ang)
    zeros_pad = jnp.zeros((s_len, LANES - NOPE_DIM - ROPE_DIM), F32)
    c_rope = jnp.concatenate([cos, cos], axis=1)
    s_rope = jnp.concatenate([-sin, sin], axis=1)
    cq = jnp.concatenate([jnp.ones((s_len, NOPE_DIM), F32), c_rope, zeros_pad], axis=1)
    ck = jnp.concatenate([jnp.zeros((s_len, NOPE_DIM), F32), c_rope, zeros_pad], axis=1)
    sk = jnp.concatenate([jnp.zeros((s_len, NOPE_DIM), F32), s_rope, zeros_pad], axis=1)
    scale_a = (NOPE_DIM + ROPE_DIM) ** -0.5 * LOG2E
    return cq * scale_a, sk * scale_a, ck, sk


def _swap_halves(w):
    half = w.shape[-1] // 2
    return jnp.concatenate([w[..., half:], w[..., :half]], axis=-1)


def _proj_ab_weights(w_in, w_uq, w_ukv):
    d = w_in.shape[0]
    wb = H_B * 2 * DH_B
    o = 0
    w_cq = w_in[:, o:o + Q_LORA]; o += Q_LORA
    w_ckv = w_in[:, o:o + KV_LORA]; o += KV_LORA
    w_kr = w_in[:, o:o + ROPE_DIM]; o += ROPE_DIM
    w_rest = w_in[:, o:o + 3 * wb]
    lead = jnp.zeros((d, NOPE_DIM), F32)
    tail = jnp.zeros((d, LANES - NOPE_DIM - ROPE_DIM), F32)
    w_kr_a = jnp.concatenate([lead, w_kr, tail], axis=1)
    w_kr_b = jnp.concatenate([lead, _swap_halves(w_kr), tail], axis=1)
    w_in_aug = jnp.concatenate([w_cq, w_ckv, w_kr_a, w_kr_b, w_rest], axis=1).astype(BF16)

    wq = w_uq.reshape(Q_LORA, H_A, NOPE_DIM + ROPE_DIM)
    wq_nope, wq_rope = wq[..., :NOPE_DIM], wq[..., NOPE_DIM:]
    zpad = jnp.zeros((Q_LORA, H_A, LANES - NOPE_DIM - ROPE_DIM), F32)
    wq1 = jnp.concatenate([wq_nope, wq_rope, zpad], axis=-1).reshape(Q_LORA, H_A * LANES)
    wq2 = jnp.concatenate([jnp.zeros_like(wq_nope), _swap_halves(wq_rope), zpad], axis=-1)
    wq12 = jnp.concatenate([wq1, wq2.reshape(Q_LORA, H_A * LANES)], axis=1).astype(BF16)

    wkv = w_ukv.reshape(KV_LORA, H_A, NOPE_DIM + V_DIM_A)
    wk = jnp.concatenate([wkv[..., :NOPE_DIM], jnp.zeros((KV_LORA, H_A, LANES - NOPE_DIM), F32)], axis=-1)
    wk = wk.reshape(KV_LORA, H_A * LANES).astype(BF16)
    wv = wkv[..., NOPE_DIM:].reshape(KV_LORA, H_A * V_DIM_A).astype(BF16)
    return w_in_aug, wq12, wk, wv


def _proj_ab(x, mod, w_in, q_norm, w_uq, kv_norm, w_ukv):
    bsz, s_len, d = x.shape
    tm = min(ROW_TILE, s_len)
    w_in_aug, wq12, wk, wv = _proj_ab_weights(w_in, w_uq, w_ukv)
    tabs = _rope_tables(s_len)
    row = lambda b, t: (b, t, 0)
    head = lambda b, t: (b, 0, t, 0)
    tab_spec = pl.BlockSpec((tm, LANES), lambda b, t: (t, 0))
    out = lambda h, w=LANES: jax.ShapeDtypeStruct((bsz, h, s_len, w), BF16)
    out_spec = lambda h, w=LANES: pl.BlockSpec((1, h, tm, w), head)
    return pl.pallas_call(
        _proj_ab_kernel,
        grid=(bsz, s_len // tm),
        in_specs=[
            pl.BlockSpec((1, tm, d), row),
            pl.BlockSpec((1, 1, 3 * d), lambda b, t: (b, 0, 0)),
            _const_spec(w_in_aug.shape),
            _const_spec((1, Q_LORA)), _const_spec((1, KV_LORA)),
            _const_spec(wq12.shape), _const_spec(wk.shape), _const_spec(wv.shape),
            tab_spec, tab_spec, tab_spec, tab_spec,
        ],
        out_specs=[out_spec(H_A), out_spec(H_A), out_spec(H_A // 2, 2 * LANES),
                   out_spec(H_B), out_spec(H_B), out_spec(H_B, 2 * LANES)],
        out_shape=[out(H_A), out(H_A), out(H_A // 2, 2 * LANES), out(H_B), out(H_B), out(H_B, 2 * LANES)],
        compiler_params=_cparams("parallel", "parallel"),
        name="proj_ab",
    )(x, mod, w_in_aug, q_norm.reshape(1, -1), kv_norm.reshape(1, -1), wq12, wk, wv, *tabs)


def _proj_cd_kernel(x_ref, mod_ref, w_ref, wf_ref, qc_ref, kc_ref, vc_ref, qd_ref, kd_ref, vd_ref, fl_ref):
    d = x_ref.shape[-1]
    u, _ = _modulate(x_ref[0], mod_ref[0], d)
    u = u.astype(BF16)
    h = _dot(u, w_ref[...])
    fl_ref[0] = _dot_nt(wf_ref[...], u)
    scale = DH_C ** -0.5 * LOG2E
    n_pair = H_C // 2
    for k, (ref, sc) in enumerate(((qc_ref, scale), (kc_ref, None), (vc_ref, None),
                                   (qd_ref, scale), (kd_ref, None), (vd_ref, None))):
        for p in range(n_pair):
            blk = h[:, (k * n_pair + p) * LANES:(k * n_pair + p + 1) * LANES]
            if sc is not None:
                blk = blk * sc
            ref[0, p, :, :LANES] = blk.astype(BF16)
            if ref is vc_ref or ref is vd_ref:
                ref[0, p, :, LANES:] = _ones_column(blk.shape[0])


def _proj_cd(x, mod, w_in):
    bsz, s_len, d = x.shape
    tm = min(ROW_TILE, s_len)
    wc, wd = H_C * DH_C, H_D * DH_D
    o = 3 * wc
    w_main = jnp.concatenate([w_in[:, :o], w_in[:, o + H_C:]], axis=1).astype(BF16)
    w_f = w_in[:, o:o + H_C].T.astype(BF16)
    n_pair = H_C // 2
    head = lambda b, t: (b, 0, t, 0)
    out = lambda w: jax.ShapeDtypeStruct((bsz, n_pair, s_len, w), BF16)
    out_spec = lambda w: pl.BlockSpec((1, n_pair, tm, w), head)
    widths = (LANES, LANES, 2 * LANES, LANES, LANES, 2 * LANES)
    return pl.pallas_call(
        _proj_cd_kernel,
        grid=(bsz, s_len // tm),
        in_specs=[
            pl.BlockSpec((1, tm, d), lambda b, t: (b, t, 0)),
            pl.BlockSpec((1, 1, 3 * d), lambda b, t: (b, 0, 0)),
            _const_spec(w_main.shape), _const_spec(w_f.shape),
        ],
        out_specs=[out_spec(w) for w in widths] + [pl.BlockSpec((1, H_C, tm), lambda b, t: (b, 0, t))],
        out_shape=[out(w) for w in widths] + [jax.ShapeDtypeStruct((bsz, H_C, s_len), F32)],
        compiler_params=_cparams("parallel", "parallel"),
        name="proj_cd",
    )(x, mod, w_main, w_f)


def _fox_scan_kernel(fl_ref, bf_ref, o_ref):
    z = fl_ref[0] + bf_ref[...]
    x = jnp.minimum(z, 0.0) - jnp.log1p(jnp.exp(-jnp.abs(z)))
    s_len = x.shape[-1]
    pos = lax.broadcasted_iota(jnp.int32, x.shape, 1)
    sh = 1
    while sh < s_len:
        x = x + jnp.where(pos >= sh, pltpu.roll(x, sh, 1), 0.0)
        sh *= 2
    o_ref[0] = -x * LOG2E


def _fox_scan(fl_t, b_f):
    bsz, h, s_len = fl_t.shape
    return pl.pallas_call(
        _fox_scan_kernel,
        grid=(bsz,),
        in_specs=[pl.BlockSpec((1, h, s_len), lambda b: (b, 0, 0)), _const_spec((h, 1))],
        out_specs=pl.BlockSpec((1, h, s_len), lambda b: (b, 0, 0)),
        out_shape=jax.ShapeDtypeStruct((bsz, h, s_len), F32),
        compiler_params=_cparams("parallel"),
        name="fox_scan",
    )(fl_t, b_f.reshape(h, 1))


def _t5_bias_kernel(tab_ref, o_ref):
    h = pl.program_id(0)
    half = T5_BUCKETS // 2
    max_exact = half // 2
    shape = (LANES, 2 * LANES)
    qq = lax.broadcasted_iota(jnp.int32, shape, 0)
    kk = lax.broadcasted_iota(jnp.int32, shape, 1)
    rel = kk - LANES - qq
    n = jnp.abs(rel)
    large = max_exact + (jnp.log(jnp.maximum(n, 1).astype(F32) / max_exact)
                         / math.log(T5_MAX_DIST / max_exact) * (half - max_exact)).astype(jnp.int32)
    large = jnp.minimum(large, half - 1)
    bucket = jnp.where(rel > 0, half, 0) + jnp.where(n < max_exact, n, large)
    val = jnp.zeros(shape, F32)
    for b in range(T5_BUCKETS):
        val = jnp.where(bucket == b, tab_ref[h, b], val)
    o_ref[0] = (val - tab_ref[h, half - 1]) * LOG2E


def _t5_bias(t5_table):
    return pl.pallas_call(
        _t5_bias_kernel,
        grid=(H_B,),
        in_specs=[pl.BlockSpec(memory_space=pltpu.SMEM)],
        out_specs=pl.BlockSpec((1, LANES, 2 * LANES), lambda h: (h, 0, 0)),
        out_shape=jax.ShapeDtypeStruct((H_B, LANES, 2 * LANES), F32),
        compiler_params=_cparams("parallel"),
        name="t5_bias",
    )(t5_table)


def _attn_kernel(*refs, mode, lam_init):
    if mode == "mla":
        q_ref, k_ref, v_ref, o_ref, m_ref, acc_ref, s_ref, p_ref, mb_ref = refs
    elif mode == "fox":
        q_ref, k_ref, v_ref, nf_ref, o_ref, m_ref, acc_ref, s_ref, p_ref, mb_ref = refs
    else:
        (q_ref, k_ref, v_ref, bias_ref, lq1_ref, lk1_ref, lq2_ref, lk2_ref, subg_ref,
         o_ref, m_ref, acc_ref, s_ref, p_ref, mb_ref) = refs
    t = o_ref.shape[1]
    i = pl.program_id(2)
    lane = lax.broadcasted_iota(jnp.int32, (t, LANES), 1)
    low = lane < LANES // 2
    if mode == "mla":
        qs = (q_ref[0, 0], q_ref[0, 1])
    else:
        q = q_ref[0, 0]
        zero = jnp.zeros_like(q)
        qs = (jnp.where(low, q, zero), jnp.where(low, zero, q))

    m_ref[...] = jnp.full(m_ref.shape, NEG_INF, F32)
    acc_ref[...] = jnp.zeros(acc_ref.shape, F32)

    def logits(s, j):
        k = k_ref[0, s, j] if mode == "mla" else k_ref[0, 0, j]
        sc = _dot_nt(qs[s], k)
        if mode == "fox":
            sc = sc + nf_ref[0, s, j]
        s_ref[s] = sc

    def softmax_pv(s, j, diag):
        if mode == "diff":
            if diag:
                s_ref[s, :LANES, :LANES] += bias_ref[0, :, LANES:]
                for r in range(1, t // LANES):
                    s_ref[s, r * LANES:(r + 1) * LANES, (r - 1) * LANES:(r + 1) * LANES] += bias_ref[0]
            else:
                is_prev = (j == i - 1).astype(F32)
                s_ref[s, :LANES, t - LANES:] += bias_ref[0, :, :LANES] * is_prev

        def visible_cols(r):
            if not diag:
                return t
            last_row = (r + 1) * STRIP - 1
            last_col = last_row if mode == "fox" else (last_row // CHUNK + 1) * CHUNK - 1
            return (last_col // LANES + 1) * LANES

        m_all = m_ref[s]
        m_parts, a_parts = [], []
        for r in range(t // STRIP):
            rows = slice(r * STRIP, (r + 1) * STRIP)
            ncol = visible_cols(r)
            if diag:
                row = lax.broadcasted_iota(jnp.int32, (STRIP, LANES), 0) + r * STRIP
                col = lax.broadcasted_iota(jnp.int32, (STRIP, LANES), 1) + (ncol - LANES)
                keep = (col <= row) if mode == "fox" else (col // CHUNK) <= (row // CHUNK)
                edge = jnp.where(keep, s_ref[s, rows, ncol - LANES:ncol], NEG_INF)
                s_ref[s, rows, ncol - LANES:ncol] = edge
            blocks = [s_ref[s, rows, c * LANES:(c + 1) * LANES] for c in range(ncol // LANES)]
            mx = functools.reduce(jnp.maximum, blocks)
            m_prev = m_all[rows]
            m_new = jnp.maximum(m_prev, jnp.max(mx, axis=1, keepdims=True))
            m_parts.append(m_new)
            a_parts.append(jnp.exp2(m_prev - m_new))
            mb_ref[s, rows] = jnp.broadcast_to(m_new, (STRIP, LANES))
        for r in range(t // STRIP):
            rows = slice(r * STRIP, (r + 1) * STRIP)
            ncol = visible_cols(r)
            mb = mb_ref[s, rows]
            for c in range(ncol // LANES):
                cols = slice(c * LANES, (c + 1) * LANES)
                p_ref[s, rows, cols] = jnp.exp2((s_ref[s, rows, cols] - mb).astype(BF16))
            if ncol < t:
                p_ref[s, rows, ncol:] = jnp.zeros((STRIP, t - ncol), BF16)
        m_ref[s] = jnp.concatenate(m_parts, axis=0)
        acc_ref[s] = jnp.concatenate(a_parts, axis=0) * acc_ref[s] + _dot(p_ref[s], v_ref[0, 0, j])

    logits(0, 0)

    def far_tile(j):
        logits(1, j)
        softmax_pv(0, j, False)
        logits(0, j + 1)
        softmax_pv(1, j, False)

    odd = i % 2

    @pl.when(odd == 1)
    def _():
        far_tile(0)

    def far_pair(jj, carry):
        far_tile(odd + 2 * jj)
        far_tile(odd + 2 * jj + 1)
        return carry

    lax.fori_loop(0, i // 2, far_pair, 0)
    logits(1, i)
    softmax_pv(0, i, True)
    softmax_pv(1, i, True)

    o0 = acc_ref[0, :, :LANES] / acc_ref[0, :, LANES:LANES + 1]
    o1 = acc_ref[1, :, :LANES] / acc_ref[1, :, LANES:LANES + 1]
    if mode == "diff":
        lam = (jnp.exp(jnp.sum(lq1_ref[...] * lk1_ref[...])) - jnp.exp(jnp.sum(lq2_ref[...] * lk2_ref[...]))
               + lam_init)
        o = _rms_norm(o0 - lam * o1, subg_ref[...]) * (1.0 - lam_init)
    else:
        o = jnp.where(low, o0, o1)
    o_ref[0] = o.astype(o_ref.dtype)


def _attention(mode, q, k, v, extra=(), lam_init=0.0):
    bsz, _, s_len, _ = q.shape
    t = min(ATTN_TILE, s_len)
    nk = s_len // t
    n_grp = v.shape[1]
    per = 2 if mode == "mla" else 1
    k5 = k.reshape(bsz, k.shape[1], nk, t, LANES)
    v5 = v.reshape(bsz, n_grp, nk, t, 2 * LANES)
    in_specs = [
        pl.BlockSpec((1, per, t, LANES), lambda b, g, i: (b, g, i, 0)),
        pl.BlockSpec((1, per, nk, t, LANES), lambda b, g, i: (b, g, 0, 0, 0)),
        pl.BlockSpec((1, 1, nk, t, 2 * LANES), lambda b, g, i: (b, g, 0, 0, 0)),
    ]
    args = [q, k5, v5]
    if mode == "fox":
        (neg_f,) = extra
        args.append(neg_f.reshape(bsz, 2 * n_grp, nk, 1, t))
        in_specs.append(pl.BlockSpec((1, 2, nk, 1, t), lambda b, g, i: (b, g, 0, 0, 0)))
    elif mode == "diff":
        bias, lq1, lk1, lq2, lk2, sub_g = extra
        args += [bias, lq1.reshape(1, -1), lk1.reshape(1, -1), lq2.reshape(1, -1), lk2.reshape(1, -1),
                 sub_g.reshape(1, -1)]
        in_specs.append(pl.BlockSpec((1, LANES, 2 * LANES), lambda b, g, i: (g, 0, 0)))
        in_specs += [_const_spec((1, DH_B))] * 4 + [_const_spec((1, 2 * DH_B))]
    scratch = [pltpu.VMEM((2, t, 1), F32), pltpu.VMEM((2, t, 2 * LANES), F32),
               pltpu.VMEM((2, t, t), F32), pltpu.VMEM((2, t, t), BF16), pltpu.VMEM((2, t, LANES), F32)]
    return pl.pallas_call(
        functools.partial(_attn_kernel, mode=mode, lam_init=lam_init),
        grid=(bsz, n_grp, s_len // t),
        in_specs=in_specs,
        out_specs=pl.BlockSpec((1, t, LANES), lambda b, g, i: (b, i, g)),
        out_shape=jax.ShapeDtypeStruct((bsz, s_len, n_grp * LANES), BF16),
        scratch_shapes=scratch,
        compiler_params=_cparams("parallel", "parallel", "arbitrary"),
        name="attn_" + mode,
    )(*args)


def _band_bias_kernel(e_ref, o_ref):
    rows, win = o_ref.shape[1], o_ref.shape[2]
    ext = jnp.broadcast_to(e_ref[0], (rows, e_ref.shape[-1]))
    o_ref[0] = pltpu.roll(ext, win + 1, 1, stride=1, stride_axis=0)[:, :win] * LOG2E


def _band_bias(rel_table):
    ext_len = BAND_WIN + BAND_GROUP
    flipped = rel_table[:, ::-1]
    left = BAND_LEAD - REL_CLIP + BAND_GROUP - 1
    ext = jnp.pad(flipped, ((0, 0), (left, ext_len - left - flipped.shape[1])), mode="edge")
    return pl.pallas_call(
        _band_bias_kernel,
        grid=(H_D,),
        in_specs=[pl.BlockSpec((1, 1, ext_len), lambda h: (h, 0, 0))],
        out_specs=pl.BlockSpec((1, BAND_GROUP, BAND_WIN), lambda h: (h, 0, 0)),
        out_shape=jax.ShapeDtypeStruct((H_D, BAND_GROUP, BAND_WIN), F32),
        compiler_params=_cparams("parallel"),
        name="band_bias",
    )(ext.reshape(H_D, 1, ext_len))


def _band_mask_kernel(o_ref):
    row = lax.broadcasted_iota(jnp.int32, o_ref.shape, 0) % BAND_GROUP
    col = lax.broadcasted_iota(jnp.int32, o_ref.shape, 1)
    first = (row // CHUNK) * CHUNK
    in_band = (col >= first) & (col < first + BAND_LEAD + CHUNK)
    o_ref[...] = jnp.where(in_band, col, -BAND_WIN * 2)


def _band_mask():
    shape = (2 * BAND_GROUP, BAND_WIN)
    return pl.pallas_call(
        _band_mask_kernel,
        out_specs=pl.BlockSpec(shape, lambda: (0, 0)),
        out_shape=jax.ShapeDtypeStruct(shape, jnp.int32),
        compiler_params=pltpu.CompilerParams(vmem_limit_bytes=VMEM_LIMIT),
        name="band_mask",
    )()


def _band_kernel(q_ref, kp_ref, kc_ref, vp_ref, vc_ref, bias_ref, vis_ref, o_ref):
    t = o_ref.shape[1]
    i = pl.program_id(2)
    lane = lax.broadcasted_iota(jnp.int32, (BAND_GROUP, LANES), 1)
    low = lane < LANES // 2
    kw = jnp.concatenate([kp_ref[0, 0], kc_ref[0, 0]], axis=0)
    vw = jnp.concatenate([vp_ref[0, 0], vc_ref[0, 0]], axis=0)
    for r in range(t // BAND_GROUP):
        q = q_ref[0, 0, r * BAND_GROUP:(r + 1) * BAND_GROUP]
        zero = jnp.zeros_like(q)
        q2 = jnp.concatenate([jnp.where(low, q, zero), jnp.where(low, zero, q)], axis=0)
        w0 = t - BAND_LEAD + r * BAND_GROUP
        keep = vis_ref[...] >= jnp.maximum(BAND_LEAD - r * BAND_GROUP - i * t, 0)
        sc = _dot_nt(q2, kw[w0:w0 + BAND_WIN]) + bias_ref[0]
        sc = jnp.where(keep, sc, NEG_INF)
        p = jnp.exp2((sc - jnp.max(sc, axis=1, keepdims=True)).astype(BF16))
        o2 = _dot(p, vw[w0:w0 + BAND_WIN])
        o2 = o2[:, :LANES] / o2[:, LANES:LANES + 1]
        o = jnp.where(low, o2[:BAND_GROUP], o2[BAND_GROUP:])
        o_ref[0, r * BAND_GROUP:(r + 1) * BAND_GROUP] = o.astype(o_ref.dtype)


def _band_attention(q, k, v, bias):
    bsz, n_grp, s_len, _ = q.shape
    t = min(BAND_TILE, s_len)
    assert t >= BAND_LEAD
    cur = lambda b, g, i: (b, g, i, 0)
    prev = lambda b, g, i: (b, g, jnp.maximum(i - 1, 0), 0)
    blk = (1, 1, t, LANES)
    vblk = (1, 1, t, 2 * LANES)
    return pl.pallas_call(
        _band_kernel,
        grid=(bsz, n_grp, s_len // t),
        in_specs=[pl.BlockSpec(blk, cur), pl.BlockSpec(blk, prev), pl.BlockSpec(blk, cur),
                  pl.BlockSpec(vblk, prev), pl.BlockSpec(vblk, cur),
                  pl.BlockSpec((1, 2 * BAND_GROUP, BAND_WIN), lambda b, g, i: (g, 0, 0)),
                  _const_spec((2 * BAND_GROUP, BAND_WIN))],
        out_specs=pl.BlockSpec((1, t, LANES), lambda b, g, i: (b, i, g)),
        out_shape=jax.ShapeDtypeStruct((bsz, s_len, n_grp * LANES), BF16),
        compiler_params=_cparams("parallel", "parallel", "parallel"),
        name="attn_band",
    )(q, k, k, v, v, bias.reshape(n_grp, 2 * BAND_GROUP, BAND_WIN), _band_mask())


def _mix_ffn_kernel(oa_ref, ob_ref, wo_ref, x_ref, mod1_ref, g1_ref, b1_ref,
                    mod_ref, wg_ref, wv_ref, cw_ref, cb_ref, wd_ref, g_ref, b_ref, o_ref, carry_ref):
    d = x_ref.shape[-1]
    tm = x_ref.shape[1]
    half = oa_ref.shape[-1]
    y = _dot(oa_ref[0], wo_ref[:half]) + _dot(ob_ref[0], wo_ref[half:])
    x = _layer_norm(DEEPNORM_ALPHA * x_ref[0] + mod1_ref[0][:, 2 * d:] * y, g1_ref[...], b1_ref[...])
    u, gate = _modulate(x, mod_ref[0], d)
    u = u.astype(BF16)
    d_ff = wg_ref.shape[1]
    n_tiles = d_ff // MXU_COLS
    edges = [(c * n_tiles // FFN_COL_CHUNKS) * MXU_COLS for c in range(FFN_COL_CHUNKS)] + [d_ff]
    seq_start = pl.program_id(1) == 0
    y = None
    for c in range(FFN_COL_CHUNKS):
        cols = slice(edges[c], edges[c + 1])
        row = lax.broadcasted_iota(jnp.int32, (tm, edges[c + 1] - edges[c]), 0)
        g = _dot(u, wg_ref[:, cols])
        val = _dot(u, wv_ref[:, cols])
        prev = jnp.where(seq_start, 0.0, carry_ref[:, cols])
        g1 = jnp.where(row == 0, prev[7:8], pltpu.roll(g, 1, 0))
        g2 = jnp.where(row == 0, prev[6:7], jnp.where(row == 1, prev[7:8], pltpu.roll(g, 2, 0)))
        carry_ref[:, cols] = g[tm - 8:]
        cw = cw_ref[:, cols]
        gc = cw[0:1] * g2 + cw[1:2] * g1 + cw[2:3] * g + cb_ref[:, cols]
        hmid = (gc * jax.nn.sigmoid(gc) * val).astype(BF16)
        yc = _dot(hmid, wd_ref[cols, :])
        y = yc if y is None else y + yc
    o_ref[0] = _layer_norm(DEEPNORM_ALPHA * x + gate * y, g_ref[...], b_ref[...])


def _mix_ffn(oa, ob, w_out, x, mod_mix, ln_g_mix, ln_b_mix,
             mod, w_gate, w_val, conv_w, conv_b, w_down, ln_g, ln_b):
    bsz, s_len, d = x.shape
    d_ff = w_gate.shape[1]
    tm = min(FFN_ROW_TILE, s_len)
    row = lambda b, t: (b, t, 0)
    per_batch = pl.BlockSpec((1, 1, 3 * d), lambda b, t: (b, 0, 0))
    once = lambda shape: pl.BlockSpec(shape, lambda b, t: (0,) * len(shape), pipeline_mode=pl.Buffered(1))
    return pl.pallas_call(
        _mix_ffn_kernel,
        grid=(bsz, s_len // tm),
        in_specs=[
            pl.BlockSpec((1, tm, oa.shape[-1]), row), pl.BlockSpec((1, tm, ob.shape[-1]), row),
            once(w_out.shape),
            pl.BlockSpec((1, tm, d), row),
            per_batch, _const_spec((1, d)), _const_spec((1, d)),
            per_batch,
            once((d, d_ff)), once((d, d_ff)),
            _const_spec((CONV_W, d_ff)), _const_spec((1, d_ff)),
            once((d_ff, d)),
            _const_spec((1, d)), _const_spec((1, d)),
        ],
        out_specs=pl.BlockSpec((1, tm, d), row),
        out_shape=jax.ShapeDtypeStruct((bsz, s_len, d), F32),
        scratch_shapes=[pltpu.VMEM((8, d_ff), F32)],
        compiler_params=_cparams("parallel", "arbitrary"),
        name="mix_ffn",
    )(oa, ob, w_out.astype(BF16), x, mod_mix, ln_g_mix.reshape(1, d), ln_b_mix.reshape(1, d),
      mod, w_gate.astype(BF16), w_val.astype(BF16), conv_w, conv_b.reshape(1, d_ff),
      w_down.astype(BF16), ln_g.reshape(1, d), ln_b.reshape(1, d))


def kernel(x, c, ada_w, ada_b, ln_g, ln_b, t5_table, ab_w_in, mla_q_norm, mla_w_uq, mla_kv_norm, mla_w_ukv, diff_lq1, diff_lk1, diff_lq2, diff_lk2, diff_sub_g, ab_w_out, cd_w_in, fox_b_f, chunk_rel_table, cd_w_out, ffn_w_gate, ffn_w_val, ffn_conv_w, ffn_conv_b, ffn_w_down):
    bsz, s_len, d = x.shape
    depth = ada_w.shape[0]
    mods = _adaln(c, ada_w, ada_b).reshape(depth, 2, bsz, 1, 3 * d)
    for i in range(depth):
        mod = mods[i, 0]
        if i % 2 == 0:
            e = i // 2
            qa, ka, va, qb, kb, vb = _proj_ab(x, mod, ab_w_in[e], mla_q_norm[e], mla_w_uq[e],
                                              mla_kv_norm[e], mla_w_ukv[e])
            o_first = _attention("mla", qa, ka, va)
            lam_init = 0.8 - 0.6 * math.exp(-0.3 * i)
            o_second = _attention("diff", qb, kb, vb,
                                  extra=(_t5_bias(t5_table), diff_lq1[e], diff_lk1[e], diff_lq2[e],
                                         diff_lk2[e], diff_sub_g[e]),
                                  lam_init=lam_init)
            w_out = ab_w_out[e]
        else:
            o = i // 2
            qc, kc, vc, qd, kd, vd, fl_t = _proj_cd(x, mod, cd_w_in[o])
            neg_f = _fox_scan(fl_t, fox_b_f[o])
            o_first = _attention("fox", qc, kc, vc, extra=(neg_f,))
            o_second = _band_attention(qd, kd, vd, _band_bias(chunk_rel_table[o]))
            w_out = cd_w_out[o]
        x = _mix_ffn(o_first, o_second, w_out, x, mod, ln_g[i, 0], ln_b[i, 0],
                     mods[i, 1], ffn_w_gate[i], ffn_w_val[i], ffn_conv_w[i], ffn_conv_b[i], ffn_w_down[i],
                     ln_g[i, 1], ln_b[i, 1])
    return x
```

```python
import functools
import math

import jax
import jax.numpy as jnp
from jax import lax
from jax.experimental import pallas as pl
from jax.experimental.pallas import tpu as pltpu

DEPTH = 2
CHUNK = 64
H_A, Q_LORA, KV_LORA, NOPE_DIM, ROPE_DIM, V_DIM_A = 8, 256, 128, 64, 32, 64
ROPE_BASE = 10000.0
H_B, DH_B, T5_BUCKETS, T5_MAX_DIST = 4, 64, 32, 128
H_C, DH_C = 8, 64
H_D, DH_D, BAND_CHUNKS, REL_CLIP = 8, 64, 8, 128
CONV_W = 3
EPS_LN = 1e-5
EPS_RMS = 1e-6
DEEPNORM_ALPHA = (2 * DEPTH) ** 0.25
NEG_INF = -1e30
LOG2E = math.log2(math.e)

MXU_COLS = 256
LANES = 128
ATTN_TILE = 1024
STRIP = 64
BAND_TILE = 512
ROW_TILE = 512
FFN_ROW_TILE = 512
FFN_COL_CHUNKS = 2
BAND_GROUP = 2 * CHUNK
BAND_LEAD = BAND_CHUNKS * CHUNK
BAND_WIN = BAND_LEAD + BAND_GROUP
VMEM_LIMIT = 56 * 1024 * 1024

BF16 = jnp.bfloat16
F32 = jnp.float32


def _cparams(*sem):
    return pltpu.CompilerParams(dimension_semantics=sem, vmem_limit_bytes=VMEM_LIMIT)


def _dot(a, b):
    return jnp.dot(a, b, preferred_element_type=F32)


def _dot_nt(a, b):
    return lax.dot_general(a, b, (((1,), (1,)), ((), ())), preferred_element_type=F32)


def _layer_norm(z, g, b):
    mu = jnp.mean(z, axis=-1, keepdims=True)
    zc = z - mu
    var = jnp.mean(zc * zc, axis=-1, keepdims=True)
    return zc * lax.rsqrt(var + EPS_LN) * g + b


def _rms_norm(z, g):
    ms = jnp.mean(z * z, axis=-1, keepdims=True)
    return z * lax.rsqrt(ms + EPS_RMS) * g


def _ones_column(rows):
    lane = lax.broadcasted_iota(jnp.int32, (rows, LANES), 1)
    return jnp.where(lane == 0, 1.0, 0.0).astype(BF16)


def _const_spec(shape):
    return pl.BlockSpec(shape, lambda *_: (0,) * len(shape))


def _adaln_kernel(c_ref, w_ref, b_ref, o_ref):
    c = c_ref[...]
    cond = c * jax.nn.sigmoid(c)
    y = jnp.dot(cond, w_ref[0], preferred_element_type=F32, precision=lax.Precision.HIGHEST)
    one = jnp.where(pl.program_id(1) > 0, 1.0, 0.0).astype(F32)
    o_ref[0] = y + b_ref[0] + one


def _adaln(c, ada_w, ada_b):
    bsz, d = c.shape
    n_sub = ada_w.shape[0] * ada_w.shape[1]
    w = ada_w.reshape(n_sub, d, 3 * d)
    b = ada_b.reshape(n_sub, 1, 3 * d)
    return pl.pallas_call(
        _adaln_kernel,
        grid=(n_sub, 3),
        in_specs=[
            pl.BlockSpec((bsz, d), lambda l, n: (0, 0)),
            pl.BlockSpec((1, d, d), lambda l, n: (l, 0, n)),
            pl.BlockSpec((1, 1, d), lambda l, n: (l, 0, n)),
        ],
        out_specs=pl.BlockSpec((1, bsz, d), lambda l, n: (l, 0, n)),
        out_shape=jax.ShapeDtypeStruct((n_sub, bsz, 3 * d), F32),
        compiler_params=_cparams("arbitrary", "arbitrary"),
        name="adaln",
    )(c, w, b)


def _modulate(x, mod, d):
    return x * mod[:, d:2 * d] + mod[:, :d], mod[:, 2 * d:]


def _proj_ab_kernel(x_ref, mod_ref, w_in_ref, qn_ref, kvn_ref, wq_ref, wk_ref, wv_ref,
                    cq_ref, sq_ref, ck_ref, sk_ref,
                    qa_ref, ka_ref, va_ref, qb_ref, kb_ref, vb_ref):
    d = x_ref.shape[-1]
    u, _ = _modulate(x_ref[0], mod_ref[0], d)
    h = _dot(u.astype(BF16), w_in_ref[...])
    o = 0
    cq = h[:, o:o + Q_LORA]; o += Q_LORA
    ckv = h[:, o:o + KV_LORA]; o += KV_LORA
    kr_a = h[:, o:o + LANES]; o += LANES
    kr_b = h[:, o:o + LANES]; o += LANES
    wb = H_B * LANES
    qb = h[:, o:o + wb]; o += wb
    kb = h[:, o:o + wb]; o += wb
    vb = h[:, o:o + wb]

    nq = _rms_norm(cq, qn_ref[...]).astype(BF16)
    q12 = _dot(nq, wq_ref[...])
    cq_t, sq_t = cq_ref[...], sq_ref[...]
    for hh in range(H_A):
        a = q12[:, hh * LANES:(hh + 1) * LANES]
        b = q12[:, (H_A + hh) * LANES:(H_A + hh + 1) * LANES]
        qa_ref[0, hh] = (a * cq_t + b * sq_t).astype(BF16)

    nkv = _rms_norm(ckv, kvn_ref[...]).astype(BF16)
    kn = _dot(nkv, wk_ref[...])
    vv = _dot(nkv, wv_ref[...])
    k_rope = kr_a * ck_ref[...] + kr_b * sk_ref[...]
    for hh in range(H_A):
        ka_ref[0, hh] = (kn[:, hh * LANES:(hh + 1) * LANES] + k_rope).astype(BF16)
    ones_col = _ones_column(vv.shape[0])
    for p in range(H_A // 2):
        va_ref[0, p, :, :LANES] = vv[:, p * LANES:(p + 1) * LANES].astype(BF16)
        va_ref[0, p, :, LANES:] = ones_col
    scale_b = DH_B ** -0.5 * LOG2E
    for hh in range(H_B):
        sl = slice(hh * LANES, (hh + 1) * LANES)
        qb_ref[0, hh] = (qb[:, sl] * scale_b).astype(BF16)
        kb_ref[0, hh] = kb[:, sl].astype(BF16)
        vb_ref[0, hh, :, :LANES] = vb[:, sl].astype(BF16)
        vb_ref[0, hh, :, LANES:] = ones_col


def _rope_tables(s_len):
    half = ROPE_DIM // 2
    inv = jnp.power(ROPE_BASE, -jnp.arange(half, dtype=F32) / half)
    ang = jnp.arange(s_len, dtype=F32)[:, None] * inv[None, :]
    cos, sin = jnp.cos(ang), jnp.sin(ang)
    zeros_pad = jnp.zeros((s_len, LANES - NOPE_DIM - ROPE_DIM), F32)
    c_rope = jnp.concatenate([cos, cos], axis=1)
    s_rope = jnp.concatenate([-sin, sin], axis=1)
    cq = jnp.concatenate([jnp.ones((s_len, NOPE_DIM), F32), c_rope, zeros_pad], axis=1)
    ck = jnp.concatenate([jnp.zeros((s_len, NOPE_DIM), F32), c_rope, zeros_pad], axis=1)
    sk = jnp.concatenate([jnp.zeros((s_len, NOPE_DIM), F32), s_rope, zeros_pad], axis=1)
    scale_a = (NOPE_DIM + ROPE_DIM) ** -0.5 * LOG2E
    return cq * scale_a, sk * scale_a, ck, sk


def _swap_halves(w):
    half = w.shape[-1] // 2
    return jnp.concatenate([w[..., half:], w[..., :half]], axis=-1)


def _proj_ab_weights(w_in, w_uq, w_ukv):
    d = w_in.shape[0]
    wb = H_B * 2 * DH_B
    o = 0
    w_cq = w_in[:, o:o + Q_LORA]; o += Q_LORA
    w_ckv = w_in[:, o:o + KV_LORA]; o += KV_LORA
    w_kr = w_in[:, o:o + ROPE_DIM]; o += ROPE_DIM
    w_rest = w_in[:, o:o + 3 * wb]
    lead = jnp.zeros((d, NOPE_DIM), F32)
    tail = jnp.zeros((d, LANES - NOPE_DIM - ROPE_DIM), F32)
    w_kr_a = jnp.concatenate([lead, w_kr, tail], axis=1)
    w_kr_b = jnp.concatenate([lead, _swap_halves(w_kr), tail], axis=1)
    w_in_aug = jnp.concatenate([w_cq, w_ckv, w_kr_a, w_kr_b, w_rest], axis=1).astype(BF16)

    wq = w_uq.reshape(Q_LORA, H_A, NOPE_DIM + ROPE_DIM)
    wq_nope, wq_rope = wq[..., :NOPE_DIM], wq[..., NOPE_DIM:]
    zpad = jnp.zeros((Q_LORA, H_A, LANES - NOPE_DIM - ROPE_DIM), F32)
    wq1 = jnp.concatenate([wq_nope, wq_rope, zpad], axis=-1).reshape(Q_LORA, H_A * LANES)
    wq2 = jnp.concatenate([jnp.zeros_like(wq_nope), _swap_halves(wq_rope), zpad], axis=-1)
    wq12 = jnp.concatenate([wq1, wq2.reshape(Q_LORA, H_A * LANES)], axis=1).astype(BF16)

    wkv = w_ukv.reshape(KV_LORA, H_A, NOPE_DIM + V_DIM_A)
    wk = jnp.concatenate([wkv[..., :NOPE_DIM], jnp.zeros((KV_LORA, H_A, LANES - NOPE_DIM), F32)], axis=-1)
    wk = wk.reshape(KV_LORA, H_A * LANES).astype(BF16)
    wv = wkv[..., NOPE_DIM:].reshape(KV_LORA, H_A * V_DIM_A).astype(BF16)
    return w_in_aug, wq12, wk, wv


def _proj_ab(x, mod, w_in, q_norm, w_uq, kv_norm, w_ukv):
    bsz, s_len, d = x.shape
    tm = min(ROW_TILE, s_len)
    w_in_aug, wq12, wk, wv = _proj_ab_weights(w_in, w_uq, w_ukv)
    tabs = _rope_tables(s_len)
    row = lambda b, t: (b, t, 0)
    head = lambda b, t: (b, 0, t, 0)
    tab_spec = pl.BlockSpec((tm, LANES), lambda b, t: (t, 0))
    out = lambda h, w=LANES: jax.ShapeDtypeStruct((bsz, h, s_len, w), BF16)
    out_spec = lambda h, w=LANES: pl.BlockSpec((1, h, tm, w), head)
    return pl.pallas_call(
        _proj_ab_kernel,
        grid=(bsz, s_len // tm),
        in_specs=[
            pl.BlockSpec((1, tm, d), row),
            pl.BlockSpec((1, 1, 3 * d), lambda b, t: (b, 0, 0)),
            _const_spec(w_in_aug.shape),
            _const_spec((1, Q_LORA)), _const_spec((1, KV_LORA)),
            _const_spec(wq12.shape), _const_spec(wk.shape), _const_spec(wv.shape),
            tab_spec, tab_spec, tab_spec, tab_spec,
        ],
        out_specs=[out_spec(H_A), out_spec(H_A), out_spec(H_A // 2, 2 * LANES),
                   out_spec(H_B), out_spec(H_B), out_spec(H_B, 2 * LANES)],
        out_shape=[out(H_A), out(H_A), out(H_A // 2, 2 * LANES), out(H_B), out(H_B), out(H_B, 2 * LANES)],
        compiler_params=_cparams("parallel", "parallel"),
        name="proj_ab",
    )(x, mod, w_in_aug, q_norm.reshape(1, -1), kv_norm.reshape(1, -1), wq12, wk, wv, *tabs)


def _proj_cd_kernel(x_ref, mod_ref, w_ref, wf_ref, qc_ref, kc_ref, vc_ref, qd_ref, kd_ref, vd_ref, fl_ref):
    d = x_ref.shape[-1]
    u, _ = _modulate(x_ref[0], mod_ref[0], d)
    u = u.astype(BF16)
    h = _dot(u, w_ref[...])
    fl_ref[0] = _dot_nt(wf_ref[...], u)
    scale = DH_C ** -0.5 * LOG2E
    n_pair = H_C // 2
    for k, (ref, sc) in enumerate(((qc_ref, scale), (kc_ref, None), (vc_ref, None),
                                   (qd_ref, scale), (kd_ref, None), (vd_ref, None))):
        for p in range(n_pair):
            blk = h[:, (k * n_pair + p) * LANES:(k * n_pair + p + 1) * LANES]
            if sc is not None:
                blk = blk * sc
            ref[0, p, :, :LANES] = blk.astype(BF16)
            if ref is vc_ref or ref is vd_ref:
                ref[0, p, :, LANES:] = _ones_column(blk.shape[0])


def _proj_cd(x, mod, w_in):
    bsz, s_len, d = x.shape
    tm = min(ROW_TILE, s_len)
    wc, wd = H_C * DH_C, H_D * DH_D
    o = 3 * wc
    w_main = jnp.concatenate([w_in[:, :o], w_in[:, o + H_C:]], axis=1).astype(BF16)
    w_f = w_in[:, o:o + H_C].T.astype(BF16)
    n_pair = H_C // 2
    head = lambda b, t: (b, 0, t, 0)
    out = lambda w: jax.ShapeDtypeStruct((bsz, n_pair, s_len, w), BF16)
    out_spec = lambda w: pl.BlockSpec((1, n_pair, tm, w), head)
    widths = (LANES, LANES, 2 * LANES, LANES, LANES, 2 * LANES)
    return pl.pallas_call(
        _proj_cd_kernel,
        grid=(bsz, s_len // tm),
        in_specs=[
            pl.BlockSpec((1, tm, d), lambda b, t: (b, t, 0)),
            pl.BlockSpec((1, 1, 3 * d), lambda b, t: (b, 0, 0)),
            _const_spec(w_main.shape), _const_spec(w_f.shape),
        ],
        out_specs=[out_spec(w) for w in widths] + [pl.BlockSpec((1, H_C, tm), lambda b, t: (b, 0, t))],
        out_shape=[out(w) for w in widths] + [jax.ShapeDtypeStruct((bsz, H_C, s_len), F32)],
        compiler_params=_cparams("parallel", "parallel"),
        name="proj_cd",
    )(x, mod, w_main, w_f)


def _fox_scan_kernel(fl_ref, bf_ref, o_ref):
    z = fl_ref[0] + bf_ref[...]
    x = jnp.minimum(z, 0.0) - jnp.log1p(jnp.exp(-jnp.abs(z)))
    s_len = x.shape[-1]
    pos = lax.broadcasted_iota(jnp.int32, x.shape, 1)
    sh = 1
    while sh < s_len:
        x = x + jnp.where(pos >= sh, pltpu.roll(x, sh, 1), 0.0)
        sh *= 2
    o_ref[0] = -x * LOG2E


def _fox_scan(fl_t, b_f):
    bsz, h, s_len = fl_t.shape
    return pl.pallas_call(
        _fox_scan_kernel,
        grid=(bsz,),
        in_specs=[pl.BlockSpec((1, h, s_len), lambda b: (b, 0, 0)), _const_spec((h, 1))],
        out_specs=pl.BlockSpec((1, h, s_len), lambda b: (b, 0, 0)),
        out_shape=jax.ShapeDtypeStruct((bsz, h, s_len), F32),
        compiler_params=_cparams("parallel"),
        name="fox_scan",
    )(fl_t, b_f.reshape(h, 1))


def _t5_bias_kernel(tab_ref, o_ref):
    h = pl.program_id(0)
    half = T5_BUCKETS // 2
    max_exact = half // 2
    shape = (LANES, 2 * LANES)
    qq = lax.broadcasted_iota(jnp.int32, shape, 0)
    kk = lax.broadcasted_iota(jnp.int32, shape, 1)
    rel = kk - LANES - qq
    n = jnp.abs(rel)
    large = max_exact + (jnp.log(jnp.maximum(n, 1).astype(F32) / max_exact)
                         / math.log(T5_MAX_DIST / max_exact) * (half - max_exact)).astype(jnp.int32)
    large = jnp.minimum(large, half - 1)
    bucket = jnp.where(rel > 0, half, 0) + jnp.where(n < max_exact, n, large)
    val = jnp.zeros(shape, F32)
    for b in range(T5_BUCKETS):
        val = jnp.where(bucket == b, tab_ref[h, b], val)
    o_ref[0] = (val - tab_ref[h, half - 1]) * LOG2E


def _t5_bias(t5_table):
    return pl.pallas_call(
        _t5_bias_kernel,
        grid=(H_B,),
        in_specs=[pl.BlockSpec(memory_space=pltpu.SMEM)],
        out_specs=pl.BlockSpec((1, LANES, 2 * LANES), lambda h: (h, 0, 0)),
        out_shape=jax.ShapeDtypeStruct((H_B, LANES, 2 * LANES), F32),
        compiler_params=_cparams("parallel"),
        name="t5_bias",
    )(t5_table)


def _attn_kernel(*refs, mode, lam_init):
    if mode == "mla":
        q_ref, k_ref, v_ref, o_ref, m_ref, acc_ref, s_ref, p_ref, mb_ref = refs
    elif mode == "fox":
        q_ref, k_ref, v_ref, nf_ref, o_ref, m_ref, acc_ref, s_ref, p_ref, mb_ref = refs
    else:
        (q_ref, k_ref, v_ref, bias_ref, lq1_ref, lk1_ref, lq2_ref, lk2_ref, subg_ref,
         o_ref, m_ref, acc_ref, s_ref, p_ref, mb_ref) = refs
    t = o_ref.shape[1]
    i = pl.program_id(2)
    lane = lax.broadcasted_iota(jnp.int32, (t, LANES), 1)
    low = lane < LANES // 2
    if mode == "mla":
        qs = (q_ref[0, 0], q_ref[0, 1])
    else:
        q = q_ref[0, 0]
        zero = jnp.zeros_like(q)
        qs = (jnp.where(low, q, zero), jnp.where(low, zero, q))

    m_ref[...] = jnp.full(m_ref.shape, NEG_INF, F32)
    acc_ref[...] = jnp.zeros(acc_ref.shape, F32)

    def logits(s, j, diag=False):
        k = k_ref[0, s, j] if mode == "mla" else k_ref[0, 0, j]
        if diag:
            h = t // 2
            top = _dot_nt(qs[s][:h], k[:h])
            bottom = _dot_nt(qs[s][h:], k)
            if mode == "fox":
                top = top + nf_ref[0, s, j][:, :h]
                bottom = bottom + nf_ref[0, s, j]
            s_ref[s, :h, :h] = top
            s_ref[s, h:] = bottom
            return
        sc = _dot_nt(qs[s], k)
        if mode == "fox":
            sc = sc + nf_ref[0, s, j]
        s_ref[s] = sc

    def softmax_pv(s, j, diag):
        if mode == "diff":
            if diag:
                s_ref[s, :LANES, :LANES] += bias_ref[0, :, LANES:]
                for r in range(1, t // LANES):
                    s_ref[s, r * LANES:(r + 1) * LANES, (r - 1) * LANES:(r + 1) * LANES] += bias_ref[0]
            else:
                is_prev = (j == i - 1).astype(F32)
                s_ref[s, :LANES, t - LANES:] += bias_ref[0, :, :LANES] * is_prev

        def visible_cols(r):
            if not diag:
                return t
            last_row = (r + 1) * STRIP - 1
            last_col = last_row if mode == "fox" else (last_row // CHUNK + 1) * CHUNK - 1
            return (last_col // LANES + 1) * LANES

        m_all = m_ref[s]
        m_parts, a_parts = [], []
        for r in range(t // STRIP):
            rows = slice(r * STRIP, (r + 1) * STRIP)
            ncol = visible_cols(r)
            if diag:
                row = lax.broadcasted_iota(jnp.int32, (STRIP, LANES), 0) + r * STRIP
                col = lax.broadcasted_iota(jnp.int32, (STRIP, LANES), 1) + (ncol - LANES)
                keep = (col <= row) if mode == "fox" else (col // CHUNK) <= (row // CHUNK)
                edge = jnp.where(keep, s_ref[s, rows, ncol - LANES:ncol], NEG_INF)
                s_ref[s, rows, ncol - LANES:ncol] = edge
            blocks = [s_ref[s, rows, c * LANES:(c + 1) * LANES] for c in range(ncol // LANES)]
            mx = functools.reduce(jnp.maximum, blocks)
            m_prev = m_all[rows]
            m_new = jnp.maximum(m_prev, jnp.max(mx, axis=1, keepdims=True))
            m_parts.append(m_new)
            a_parts.append(jnp.exp2(m_prev - m_new))
            mb_ref[s, rows] = jnp.broadcast_to(m_new, (STRIP, LANES))
        for r in range(t // STRIP):
            rows = slice(r * STRIP, (r + 1) * STRIP)
            ncol = visible_cols(r)
            mb = mb_ref[s, rows]
            for c in range(ncol // LANES):
                cols = slice(c * LANES, (c + 1) * LANES)
                p_ref[s, rows, cols] = jnp.exp2((s_ref[s, rows, cols] - mb).astype(BF16))
            limit = t // 2 if (diag and (r + 1) * STRIP <= t // 2) else t
            if ncol < limit:
                p_ref[s, rows, ncol:limit] = jnp.zeros((STRIP, limit - ncol), BF16)
        m_ref[s] = jnp.concatenate(m_parts, axis=0)
        alpha = jnp.concatenate(a_parts, axis=0)
        if diag:
            h = t // 2
            v = v_ref[0, 0, j]
            acc_ref[s, :h] = alpha[:h] * acc_ref[s, :h] + _dot(p_ref[s, :h, :h], v[:h])
            acc_ref[s, h:] = alpha[h:] * acc_ref[s, h:] + _dot(p_ref[s, h:], v)
        else:
            acc_ref[s] = alpha * acc_ref[s] + _dot(p_ref[s], v_ref[0, 0, j])

    logits(0, 0)

    def far_tile(j):
        logits(1, j)
        softmax_pv(0, j, False)
        logits(0, j + 1)
        softmax_pv(1, j, False)

    odd = i % 2

    @pl.when(odd == 1)
    def _():
        far_tile(0)

    def far_pair(jj, carry):
        far_tile(odd + 2 * jj)
        far_tile(odd + 2 * jj + 1)
        return carry

    lax.fori_loop(0, i // 2, far_pair, 0)
    logits(1, i, diag=True)
    softmax_pv(0, i, True)
    softmax_pv(1, i, True)

    o0 = acc_ref[0, :, :LANES] / acc_ref[0, :, LANES:LANES + 1]
    o1 = acc_ref[1, :, :LANES] / acc_ref[1, :, LANES:LANES + 1]
    if mode == "diff":
        lam = (jnp.exp(jnp.sum(lq1_ref[...] * lk1_ref[...])) - jnp.exp(jnp.sum(lq2_ref[...] * lk2_ref[...]))
               + lam_init)
        o = _rms_norm(o0 - lam * o1, subg_ref[...]) * (1.0 - lam_init)
    else:
        o = jnp.where(low, o0, o1)
    o_ref[0] = o.astype(o_ref.dtype)


def _attention(mode, q, k, v, extra=(), lam_init=0.0):
    bsz, _, s_len, _ = q.shape
    t = min(ATTN_TILE, s_len)
    nk = s_len // t
    n_grp = v.shape[1]
    per = 2 if mode == "mla" else 1
    k5 = k.reshape(bsz, k.shape[1], nk, t, LANES)
    v5 = v.reshape(bsz, n_grp, nk, t, 2 * LANES)
    in_specs = [
        pl.BlockSpec((1, per, t, LANES), lambda b, g, i: (b, g, i, 0)),
        pl.BlockSpec((1, per, nk, t, LANES), lambda b, g, i: (b, g, 0, 0, 0)),
        pl.BlockSpec((1, 1, nk, t, 2 * LANES), lambda b, g, i: (b, g, 0, 0, 0)),
    ]
    args = [q, k5, v5]
    if mode == "fox":
        (neg_f,) = extra
        args.append(neg_f.reshape(bsz, 2 * n_grp, nk, 1, t))
        in_specs.append(pl.BlockSpec((1, 2, nk, 1, t), lambda b, g, i: (b, g, 0, 0, 0)))
    elif mode == "diff":
        bias, lq1, lk1, lq2, lk2, sub_g = extra
        args += [bias, lq1.reshape(1, -1), lk1.reshape(1, -1), lq2.reshape(1, -1), lk2.reshape(1, -1),
                 sub_g.reshape(1, -1)]
        in_specs.append(pl.BlockSpec((1, LANES, 2 * LANES), lambda b, g, i: (g, 0, 0)))
        in_specs += [_const_spec((1, DH_B))] * 4 + [_const_spec((1, 2 * DH_B))]
    scratch = [pltpu.VMEM((2, t, 1), F32), pltpu.VMEM((2, t, 2 * LANES), F32),
               pltpu.VMEM((2, t, t), F32), pltpu.VMEM((2, t, t), BF16), pltpu.VMEM((2, t, LANES), F32)]
    return pl.pallas_call(
        functools.partial(_attn_kernel, mode=mode, lam_init=lam_init),
        grid=(bsz, n_grp, s_len // t),
        in_specs=in_specs,
        out_specs=pl.BlockSpec((1, t, LANES), lambda b, g, i: (b, i, g)),
        out_shape=jax.ShapeDtypeStruct((bsz, s_len, n_grp * LANES), BF16),
        scratch_shapes=scratch,
        compiler_params=_cparams("parallel", "parallel", "arbitrary"),
        name="attn_" + mode,
    )(*args)


def _band_bias_kernel(e_ref, o_ref):
    rows, win = o_ref.shape[1], o_ref.shape[2]
    ext = jnp.broadcast_to(e_ref[0], (rows, e_ref.shape[-1]))
    o_ref[0] = pltpu.roll(ext, win + 1, 1, stride=1, stride_axis=0)[:, :win] * LOG2E


def _band_bias(rel_table):
    ext_len = BAND_WIN + BAND_GROUP
    flipped = rel_table[:, ::-1]
    left = BAND_LEAD - REL_CLIP + BAND_GROUP - 1
    ext = jnp.pad(flipped, ((0, 0), (left, ext_len - left - flipped.shape[1])), mode="edge")
    return pl.pallas_call(
        _band_bias_kernel,
        grid=(H_D,),
        in_specs=[pl.BlockSpec((1, 1, ext_len), lambda h: (h, 0, 0))],
        out_specs=pl.BlockSpec((1, BAND_GROUP, BAND_WIN), lambda h: (h, 0, 0)),
        out_shape=jax.ShapeDtypeStruct((H_D, BAND_GROUP, BAND_WIN), F32),
        compiler_params=_cparams("parallel"),
        name="band_bias",
    )(ext.reshape(H_D, 1, ext_len))


def _band_mask_kernel(o_ref):
    row = lax.broadcasted_iota(jnp.int32, o_ref.shape, 0) % BAND_GROUP
    col = lax.broadcasted_iota(jnp.int32, o_ref.shape, 1)
    first = (row // CHUNK) * CHUNK
    in_band = (col >= first) & (col < first + BAND_LEAD + CHUNK)
    o_ref[...] = jnp.where(in_band, col, -BAND_WIN * 2)


def _band_mask():
    shape = (2 * BAND_GROUP, BAND_WIN)
    return pl.pallas_call(
        _band_mask_kernel,
        out_specs=pl.BlockSpec(shape, lambda: (0, 0)),
        out_shape=jax.ShapeDtypeStruct(shape, jnp.int32),
        compiler_params=pltpu.CompilerParams(vmem_limit_bytes=VMEM_LIMIT),
        name="band_mask",
    )()


def _band_kernel(q_ref, kp_ref, kc_ref, vp_ref, vc_ref, bias_ref, vis_ref, o_ref):
    t = o_ref.shape[1]
    i = pl.program_id(2)
    lane = lax.broadcasted_iota(jnp.int32, (BAND_GROUP, LANES), 1)
    low = lane < LANES // 2
    kw = jnp.concatenate([kp_ref[0, 0], kc_ref[0, 0]], axis=0)
    vw = jnp.concatenate([vp_ref[0, 0], vc_ref[0, 0]], axis=0)
    for r in range(t // BAND_GROUP):
        q = q_ref[0, 0, r * BAND_GROUP:(r + 1) * BAND_GROUP]
        zero = jnp.zeros_like(q)
        q2 = jnp.concatenate([jnp.where(low, q, zero), jnp.where(low, zero, q)], axis=0)
        w0 = t - BAND_LEAD + r * BAND_GROUP
        keep = vis_ref[...] >= jnp.maximum(BAND_LEAD - r * BAND_GROUP - i * t, 0)
        sc = _dot_nt(q2, kw[w0:w0 + BAND_WIN]) + bias_ref[0]
        sc = jnp.where(keep, sc, NEG_INF)
        p = jnp.exp2((sc - jnp.max(sc, axis=1, keepdims=True)).astype(BF16))
        o2 = _dot(p, vw[w0:w0 + BAND_WIN])
        o2 = o2[:, :LANES] / o2[:, LANES:LANES + 1]
        o = jnp.where(low, o2[:BAND_GROUP], o2[BAND_GROUP:])
        o_ref[0, r * BAND_GROUP:(r + 1) * BAND_GROUP] = o.astype(o_ref.dtype)


def _band_attention(q, k, v, bias):
    bsz, n_grp, s_len, _ = q.shape
    t = min(BAND_TILE, s_len)
    assert t >= BAND_LEAD
    cur = lambda b, g, i: (b, g, i, 0)
    prev = lambda b, g, i: (b, g, jnp.maximum(i - 1, 0), 0)
    blk = (1, 1, t, LANES)
    vblk = (1, 1, t, 2 * LANES)
    return pl.pallas_call(
        _band_kernel,
        grid=(bsz, n_grp, s_len // t),
        in_specs=[pl.BlockSpec(blk, cur), pl.BlockSpec(blk, prev), pl.BlockSpec(blk, cur),
                  pl.BlockSpec(vblk, prev), pl.BlockSpec(vblk, cur),
                  pl.BlockSpec((1, 2 * BAND_GROUP, BAND_WIN), lambda b, g, i: (g, 0, 0)),
                  _const_spec((2 * BAND_GROUP, BAND_WIN))],
        out_specs=pl.BlockSpec((1, t, LANES), lambda b, g, i: (b, i, g)),
        out_shape=jax.ShapeDtypeStruct((bsz, s_len, n_grp * LANES), BF16),
        compiler_params=_cparams("parallel", "parallel", "parallel"),
        name="attn_band",
    )(q, k, k, v, v, bias.reshape(n_grp, 2 * BAND_GROUP, BAND_WIN), _band_mask())


def _mix_ffn_kernel(oa_ref, ob_ref, wo_ref, x_ref, mod1_ref, g1_ref, b1_ref,
                    mod_ref, wg_ref, wv_ref, cw_ref, cb_ref, wd_ref, g_ref, b_ref, o_ref, carry_ref):
    d = x_ref.shape[-1]
    tm = x_ref.shape[1]
    half = oa_ref.shape[-1]
    y = _dot(oa_ref[0], wo_ref[:half]) + _dot(ob_ref[0], wo_ref[half:])
    x = _layer_norm(DEEPNORM_ALPHA * x_ref[0] + mod1_ref[0][:, 2 * d:] * y, g1_ref[...], b1_ref[...])
    u, gate = _modulate(x, mod_ref[0], d)
    u = u.astype(BF16)
    d_ff = wg_ref.shape[1]
    n_tiles = d_ff // MXU_COLS
    edges = [(c * n_tiles // FFN_COL_CHUNKS) * MXU_COLS for c in range(FFN_COL_CHUNKS)] + [d_ff]
    seq_start = pl.program_id(1) == 0
    y = None
    for c in range(FFN_COL_CHUNKS):
        cols = slice(edges[c], edges[c + 1])
        row = lax.broadcasted_iota(jnp.int32, (tm, edges[c + 1] - edges[c]), 0)
        g = _dot(u, wg_ref[:, cols])
        val = _dot(u, wv_ref[:, cols])
        prev = jnp.where(seq_start, 0.0, carry_ref[:, cols])
        g1 = jnp.where(row == 0, prev[7:8], pltpu.roll(g, 1, 0))
        g2 = jnp.where(row == 0, prev[6:7], jnp.where(row == 1, prev[7:8], pltpu.roll(g, 2, 0)))
        carry_ref[:, cols] = g[tm - 8:]
        cw = cw_ref[:, cols]
        gc = cw[0:1] * g2 + cw[1:2] * g1 + cw[2:3] * g + cb_ref[:, cols]
        hmid = (gc * jax.nn.sigmoid(gc) * val).astype(BF16)
        yc = _dot(hmid, wd_ref[cols, :])
        y = yc if y is None else y + yc
    o_ref[0] = _layer_norm(DEEPNORM_ALPHA * x + gate * y, g_ref[...], b_ref[...])


def _mix_ffn(oa, ob, w_out, x, mod_mix, ln_g_mix, ln_b_mix,
             mod, w_gate, w_val, conv_w, conv_b, w_down, ln_g, ln_b):
    bsz, s_len, d = x.shape
    d_ff = w_gate.shape[1]
    tm = min(FFN_ROW_TILE, s_len)
    row = lambda b, t: (b, t, 0)
    per_batch = pl.BlockSpec((1, 1, 3 * d), lambda b, t: (b, 0, 0))
    once = lambda shape: pl.BlockSpec(shape, lambda b, t: (0,) * len(shape), pipeline_mode=pl.Buffered(1))
    return pl.pallas_call(
        _mix_ffn_kernel,
        grid=(bsz, s_len // tm),
        in_specs=[
            pl.BlockSpec((1, tm, oa.shape[-1]), row), pl.BlockSpec((1, tm, ob.shape[-1]), row),
            once(w_out.shape),
            pl.BlockSpec((1, tm, d), row),
            per_batch, _const_spec((1, d)), _const_spec((1, d)),
            per_batch,
            once((d, d_ff)), once((d, d_ff)),
            _const_spec((CONV_W, d_ff)), _const_spec((1, d_ff)),
            once((d_ff, d)),
            _const_spec((1, d)), _const_spec((1, d)),
        ],
        out_specs=pl.BlockSpec((1, tm, d), row),
        out_shape=jax.ShapeDtypeStruct((bsz, s_len, d), F32),
        scratch_shapes=[pltpu.VMEM((8, d_ff), F32)],
        compiler_params=_cparams("parallel", "arbitrary"),
        name="mix_ffn",
    )(oa, ob, w_out.astype(BF16), x, mod_mix, ln_g_mix.reshape(1, d), ln_b_mix.reshape(1, d),
      mod, w_gate.astype(BF16), w_val.astype(BF16), conv_w, conv_b.reshape(1, d_ff),
      w_down.astype(BF16), ln_g.reshape(1, d), ln_b.reshape(1, d))


def kernel(x, c, ada_w, ada_b, ln_g, ln_b, t5_table, ab_w_in, mla_q_norm, mla_w_uq, mla_kv_norm, mla_w_ukv, diff_lq1, diff_lk1, diff_lq2, diff_lk2, diff_sub_g, ab_w_out, cd_w_in, fox_b_f, chunk_rel_table, cd_w_out, ffn_w_gate, ffn_w_val, ffn_conv_w, ffn_conv_b, ffn_w_down):
    bsz, s_len, d = x.shape
    depth = ada_w.shape[0]
    mods = _adaln(c, ada_w, ada_b).reshape(depth, 2, bsz, 1, 3 * d)
    for i in range(depth):
        mod = mods[i, 0]
        if i % 2 == 0:
            e = i // 2
            qa, ka, va, qb, kb, vb = _proj_ab(x, mod, ab_w_in[e], mla_q_norm[e], mla_w_uq[e],
                                              mla_kv_norm[e], mla_w_ukv[e])
            o_first = _attention("mla", qa, ka, va)
            lam_init = 0.8 - 0.6 * math.exp(-0.3 * i)
            o_second = _attention("diff", qb, kb, vb,
                                  extra=(_t5_bias(t5_table), diff_lq1[e], diff_lk1[e], diff_lq2[e],
                                         diff_lk2[e], diff_sub_g[e]),
                                  lam_init=lam_init)
            w_out = ab_w_out[e]
        else:
            o = i // 2
            qc, kc, vc, qd, kd, vd, fl_t = _proj_cd(x, mod, cd_w_in[o])
            neg_f = _fox_scan(fl_t, fox_b_f[o])
            o_first = _attention("fox", qc, kc, vc, extra=(neg_f,))
            o_second = _band_attention(qd, kd, vd, _band_bias(chunk_rel_table[o]))
            w_out = cd_w_out[o]
        x = _mix_ffn(o_first, o_second, w_out, x, mod, ln_g[i, 0], ln_b[i, 0],
                     mods[i, 1], ffn_w_gate[i], ffn_w_val[i], ffn_conv_w[i], ffn_conv_b[i], ffn_w_down[i],
                     ln_g[i, 1], ln_b[i, 1])
    return x
```

```python
import functools
import math

import jax
import jax.numpy as jnp
from jax import lax
from jax.experimental import pallas as pl
from jax.experimental.pallas import tpu as pltpu

DEPTH = 2
CHUNK = 64
H_A, Q_LORA, KV_LORA, NOPE_DIM, ROPE_DIM, V_DIM_A = 8, 256, 128, 64, 32, 64
ROPE_BASE = 10000.0
H_B, DH_B, T5_BUCKETS, T5_MAX_DIST = 4, 64, 32, 128
H_C, DH_C = 8, 64
H_D, DH_D, BAND_CHUNKS, REL_CLIP = 8, 64, 8, 128
CONV_W = 3
EPS_LN = 1e-5
EPS_RMS = 1e-6
DEEPNORM_ALPHA = (2 * DEPTH) ** 0.25
NEG_INF = -1e30
LOG2E = math.log2(math.e)

MXU_COLS = 256
LANES = 128
ATTN_TILE = 1024
STRIP = 64
BAND_TILE = 1024
ROW_TILE = 512
FFN_ROW_TILE = 512
FFN_COL_CHUNKS = 2
BAND_GROUP = 2 * CHUNK
BAND_LEAD = BAND_CHUNKS * CHUNK
BAND_WIN = BAND_LEAD + BAND_GROUP
VMEM_LIMIT = 56 * 1024 * 1024

BF16 = jnp.bfloat16
F32 = jnp.float32


def _cparams(*sem):
    return pltpu.CompilerParams(dimension_semantics=sem, vmem_limit_bytes=VMEM_LIMIT)


def _dot(a, b):
    return jnp.dot(a, b, preferred_element_type=F32)


def _dot_nt(a, b):
    return lax.dot_general(a, b, (((1,), (1,)), ((), ())), preferred_element_type=F32)


def _layer_norm(z, g, b):
    mu = jnp.mean(z, axis=-1, keepdims=True)
    zc = z - mu
    var = jnp.mean(zc * zc, axis=-1, keepdims=True)
    return zc * lax.rsqrt(var + EPS_LN) * g + b


def _rms_norm(z, g):
    ms = jnp.mean(z * z, axis=-1, keepdims=True)
    return z * lax.rsqrt(ms + EPS_RMS) * g


def _ones_column(rows):
    lane = lax.broadcasted_iota(jnp.int32, (rows, LANES), 1)
    return jnp.where(lane == 0, 1.0, 0.0).astype(BF16)


def _const_spec(shape):
    return pl.BlockSpec(shape, lambda *_: (0,) * len(shape))


def _adaln_kernel(c_ref, w_ref, b_ref, o_ref):
    c = c_ref[...]
    cond = c * jax.nn.sigmoid(c)
    y = jnp.dot(cond, w_ref[0], preferred_element_type=F32, precision=lax.Precision.HIGHEST)
    col = lax.broadcasted_iota(jnp.int32, y.shape, 1)
    o_ref[0] = y + b_ref[0] + jnp.where(col >= c.shape[1], 1.0, 0.0)


def _adaln(c, ada_w, ada_b):
    bsz, d = c.shape
    n_sub = ada_w.shape[0] * ada_w.shape[1]
    w = ada_w.reshape(n_sub, d, 3 * d)
    b = ada_b.reshape(n_sub, 1, 3 * d)
    return pl.pallas_call(
        _adaln_kernel,
        grid=(n_sub,),
        in_specs=[
            pl.BlockSpec((bsz, d), lambda l: (0, 0)),
            pl.BlockSpec((1, d, 3 * d), lambda l: (l, 0, 0)),
            pl.BlockSpec((1, 1, 3 * d), lambda l: (l, 0, 0)),
        ],
        out_specs=pl.BlockSpec((1, bsz, 3 * d), lambda l: (l, 0, 0)),
        out_shape=jax.ShapeDtypeStruct((n_sub, bsz, 3 * d), F32),
        compiler_params=_cparams("arbitrary"),
        name="adaln",
    )(c, w, b)


def _modulate(x, mod, d):
    return x * mod[:, d:2 * d] + mod[:, :d], mod[:, 2 * d:]


def _proj_ab_kernel(x_ref, mod_ref, w_in_ref, qn_ref, kvn_ref, wq_ref, wk_ref, wv_ref,
                    cq_ref, sq_ref, ck_ref, sk_ref,
                    qa_ref, ka_ref, va_ref, qb_ref, kb_ref, vb_ref):
    d = x_ref.shape[-1]
    u, _ = _modulate(x_ref[0], mod_ref[0], d)
    h = _dot(u.astype(BF16), w_in_ref[...])
    o = 0
    cq = h[:, o:o + Q_LORA]; o += Q_LORA
    ckv = h[:, o:o + KV_LORA]; o += KV_LORA
    kr_a = h[:, o:o + LANES]; o += LANES
    kr_b = h[:, o:o + LANES]; o += LANES
    wb = H_B * LANES
    qb = h[:, o:o + wb]; o += wb
    kb = h[:, o:o + wb]; o += wb
    vb = h[:, o:o + wb]

    nq = _rms_norm(cq, qn_ref[...]).astype(BF16)
    q12 = _dot(nq, wq_ref[...])
    cq_t, sq_t = cq_ref[...], sq_ref[...]
    for hh in range(H_A):
        a = q12[:, hh * LANES:(hh + 1) * LANES]
        b = q12[:, (H_A + hh) * LANES:(H_A + hh + 1) * LANES]
        qa_ref[0, hh] = (a * cq_t + b * sq_t).astype(BF16)

    nkv = _rms_norm(ckv, kvn_ref[...]).astype(BF16)
    kn = _dot(nkv, wk_ref[...])
    vv = _dot(nkv, wv_ref[...])
    k_rope = kr_a * ck_ref[...] + kr_b * sk_ref[...]
    for hh in range(H_A):
        ka_ref[0, hh] = (kn[:, hh * LANES:(hh + 1) * LANES] + k_rope).astype(BF16)
    ones_col = _ones_column(vv.shape[0])
    for p in range(H_A // 2):
        va_ref[0, p, :, :LANES] = vv[:, p * LANES:(p + 1) * LANES].astype(BF16)
        va_ref[0, p, :, LANES:] = ones_col
    scale_b = DH_B ** -0.5 * LOG2E
    for hh in range(H_B):
        sl = slice(hh * LANES, (hh + 1) * LANES)
        qb_ref[0, hh] = (qb[:, sl] * scale_b).astype(BF16)
        kb_ref[0, hh] = kb[:, sl].astype(BF16)
        vb_ref[0, hh, :, :LANES] = vb[:, sl].astype(BF16)
        vb_ref[0, hh, :, LANES:] = ones_col


def _rope_tables(s_len):
    half = ROPE_DIM // 2
    inv = jnp.power(ROPE_BASE, -jnp.arange(half, dtype=F32) / half)
    ang = jnp.arange(s_len, dtype=F32)[:, None] * inv[None, :]
    cos, sin = jnp.cos(ang), jnp.sin(ang)
    zeros_pad = jnp.zeros((s_len, LANES - NOPE_DIM - ROPE_DIM), F32)
    c_rope = jnp.concatenate([cos, cos], axis=1)
    s_rope = jnp.concatenate([-sin, sin], axis=1)
    cq = jnp.concatenate([jnp.ones((s_len, NOPE_DIM), F32), c_rope, zeros_pad], axis=1)
    ck = jnp.concatenate([jnp.zeros((s_len, NOPE_DIM), F32), c_rope, zeros_pad], axis=1)
    sk = jnp.concatenate([jnp.zeros((s_len, NOPE_DIM), F32), s_rope, zeros_pad], axis=1)
    scale_a = (NOPE_DIM + ROPE_DIM) ** -0.5 * LOG2E
    return cq * scale_a, sk * scale_a, ck, sk


def _swap_halves(w):
    half = w.shape[-1] // 2
    return jnp.concatenate([w[..., half:], w[..., :half]], axis=-1)


def _proj_ab_weights(w_in, w_uq, w_ukv):
    d = w_in.shape[0]
    wb = H_B * 2 * DH_B
    o = 0
    w_cq = w_in[:, o:o + Q_LORA]; o += Q_LORA
    w_ckv = w_in[:, o:o + KV_LORA]; o += KV_LORA
    w_kr = w_in[:, o:o + ROPE_DIM]; o += ROPE_DIM
    w_rest = w_in[:, o:o + 3 * wb]
    lead = jnp.zeros((d, NOPE_DIM), F32)
    tail = jnp.zeros((d, LANES - NOPE_DIM - ROPE_DIM), F32)
    w_kr_a = jnp.concatenate([lead, w_kr, tail], axis=1)
    w_kr_b = jnp.concatenate([lead, _swap_halves(w_kr), tail], axis=1)
    w_in_aug = jnp.concatenate([w_cq, w_ckv, w_kr_a, w_kr_b, w_rest], axis=1).astype(BF16)

    wq = w_uq.reshape(Q_LORA, H_A, NOPE_DIM + ROPE_DIM)
    wq_nope, wq_rope = wq[..., :NOPE_DIM], wq[..., NOPE_DIM:]
    zpad = jnp.zeros((Q_LORA, H_A, LANES - NOPE_DIM - ROPE_DIM), F32)
    wq1 = jnp.concatenate([wq_nope, wq_rope, zpad], axis=-1).reshape(Q_LORA, H_A * LANES)
    wq2 = jnp.concatenate([jnp.zeros_like(wq_nope), _swap_halves(wq_rope), zpad], axis=-1)
    wq12 = jnp.concatenate([wq1, wq2.reshape(Q_LORA, H_A * LANES)], axis=1).astype(BF16)

    wkv = w_ukv.reshape(KV_LORA, H_A, NOPE_DIM + V_DIM_A)
    wk = jnp.concatenate([wkv[..., :NOPE_DIM], jnp.zeros((KV_LORA, H_A, LANES - NOPE_DIM), F32)], axis=-1)
    wk = wk.reshape(KV_LORA, H_A * LANES).astype(BF16)
    wv = wkv[..., NOPE_DIM:].reshape(KV_LORA, H_A * V_DIM_A).astype(BF16)
    return w_in_aug, wq12, wk, wv


def _proj_ab(x, mod, w_in, q_norm, w_uq, kv_norm, w_ukv):
    bsz, s_len, d = x.shape
    tm = min(ROW_TILE, s_len)
    w_in_aug, wq12, wk, wv = _proj_ab_weights(w_in, w_uq, w_ukv)
    tabs = _rope_tables(s_len)
    row = lambda b, t: (b, t, 0)
    head = lambda b, t: (b, 0, t, 0)
    tab_spec = pl.BlockSpec((tm, LANES), lambda b, t: (t, 0))
    out = lambda h, w=LANES: jax.ShapeDtypeStruct((bsz, h, s_len, w), BF16)
    out_spec = lambda h, w=LANES: pl.BlockSpec((1, h, tm, w), head)
    return pl.pallas_call(
        _proj_ab_kernel,
        grid=(bsz, s_len // tm),
        in_specs=[
            pl.BlockSpec((1, tm, d), row),
            pl.BlockSpec((1, 1, 3 * d), lambda b, t: (b, 0, 0)),
            _const_spec(w_in_aug.shape),
            _const_spec((1, Q_LORA)), _const_spec((1, KV_LORA)),
            _const_spec(wq12.shape), _const_spec(wk.shape), _const_spec(wv.shape),
            tab_spec, tab_spec, tab_spec, tab_spec,
        ],
        out_specs=[out_spec(H_A), out_spec(H_A), out_spec(H_A // 2, 2 * LANES),
                   out_spec(H_B), out_spec(H_B), out_spec(H_B, 2 * LANES)],
        out_shape=[out(H_A), out(H_A), out(H_A // 2, 2 * LANES), out(H_B), out(H_B), out(H_B, 2 * LANES)],
        compiler_params=_cparams("parallel", "parallel"),
        name="proj_ab",
    )(x, mod, w_in_aug, q_norm.reshape(1, -1), kv_norm.reshape(1, -1), wq12, wk, wv, *tabs)


def _proj_cd_kernel(x_ref, mod_ref, w_ref, wf_ref, qc_ref, kc_ref, vc_ref, qd_ref, kd_ref, vd_ref, fl_ref):
    d = x_ref.shape[-1]
    u, _ = _modulate(x_ref[0], mod_ref[0], d)
    u = u.astype(BF16)
    h = _dot(u, w_ref[...])
    fl_ref[0] = _dot_nt(wf_ref[...], u)
    scale = DH_C ** -0.5 * LOG2E
    n_pair = H_C // 2
    for k, (ref, sc) in enumerate(((qc_ref, scale), (kc_ref, None), (vc_ref, None),
                                   (qd_ref, scale), (kd_ref, None), (vd_ref, None))):
        for p in range(n_pair):
            blk = h[:, (k * n_pair + p) * LANES:(k * n_pair + p + 1) * LANES]
            if sc is not None:
                blk = blk * sc
            ref[0, p, :, :LANES] = blk.astype(BF16)
            if ref is vc_ref or ref is vd_ref:
                ref[0, p, :, LANES:] = _ones_column(blk.shape[0])


def _proj_cd(x, mod, w_in):
    bsz, s_len, d = x.shape
    tm = min(ROW_TILE, s_len)
    wc, wd = H_C * DH_C, H_D * DH_D
    o = 3 * wc
    w_main = jnp.concatenate([w_in[:, :o], w_in[:, o + H_C:]], axis=1).astype(BF16)
    w_f = w_in[:, o:o + H_C].T.astype(BF16)
    n_pair = H_C // 2
    head = lambda b, t: (b, 0, t, 0)
    out = lambda w: jax.ShapeDtypeStruct((bsz, n_pair, s_len, w), BF16)
    out_spec = lambda w: pl.BlockSpec((1, n_pair, tm, w), head)
    widths = (LANES, LANES, 2 * LANES, LANES, LANES, 2 * LANES)
    return pl.pallas_call(
        _proj_cd_kernel,
        grid=(bsz, s_len // tm),
        in_specs=[
            pl.BlockSpec((1, tm, d), lambda b, t: (b, t, 0)),
            pl.BlockSpec((1, 1, 3 * d), lambda b, t: (b, 0, 0)),
            _const_spec(w_main.shape), _const_spec(w_f.shape),
        ],
        out_specs=[out_spec(w) for w in widths] + [pl.BlockSpec((1, H_C, tm), lambda b, t: (b, 0, t))],
        out_shape=[out(w) for w in widths] + [jax.ShapeDtypeStruct((bsz, H_C, s_len), F32)],
        compiler_params=_cparams("parallel", "parallel"),
        name="proj_cd",
    )(x, mod, w_main, w_f)


def _fox_scan_kernel(fl_ref, bf_ref, o_ref):
    z = fl_ref[0] + bf_ref[...]
    x = jnp.minimum(z, 0.0) - jnp.log1p(jnp.exp(-jnp.abs(z)))
    s_len = x.shape[-1]
    pos = lax.broadcasted_iota(jnp.int32, x.shape, 1)
    sh = 1
    while sh < s_len:
        x = x + jnp.where(pos >= sh, pltpu.roll(x, sh, 1), 0.0)
        sh *= 2
    o_ref[0] = -x * LOG2E


def _fox_scan(fl_t, b_f):
    bsz, h, s_len = fl_t.shape
    return pl.pallas_call(
        _fox_scan_kernel,
        grid=(bsz,),
        in_specs=[pl.BlockSpec((1, h, s_len), lambda b: (b, 0, 0)), _const_spec((h, 1))],
        out_specs=pl.BlockSpec((1, h, s_len), lambda b: (b, 0, 0)),
        out_shape=jax.ShapeDtypeStruct((bsz, h, s_len), F32),
        compiler_params=_cparams("parallel"),
        name="fox_scan",
    )(fl_t, b_f.reshape(h, 1))


def _t5_bias_kernel(tab_ref, o_ref):
    h = pl.program_id(0)
    half = T5_BUCKETS // 2
    max_exact = half // 2
    shape = (LANES, 2 * LANES)
    qq = lax.broadcasted_iota(jnp.int32, shape, 0)
    kk = lax.broadcasted_iota(jnp.int32, shape, 1)
    rel = kk - LANES - qq
    n = jnp.abs(rel)
    large = max_exact + (jnp.log(jnp.maximum(n, 1).astype(F32) / max_exact)
                         / math.log(T5_MAX_DIST / max_exact) * (half - max_exact)).astype(jnp.int32)
    large = jnp.minimum(large, half - 1)
    bucket = jnp.where(rel > 0, half, 0) + jnp.where(n < max_exact, n, large)
    val = jnp.zeros(shape, F32)
    for b in range(T5_BUCKETS):
        val = jnp.where(bucket == b, tab_ref[h, b], val)
    o_ref[0] = (val - tab_ref[h, half - 1]) * LOG2E


def _t5_bias(t5_table):
    return pl.pallas_call(
        _t5_bias_kernel,
        grid=(H_B,),
        in_specs=[pl.BlockSpec(memory_space=pltpu.SMEM)],
        out_specs=pl.BlockSpec((1, LANES, 2 * LANES), lambda h: (h, 0, 0)),
        out_shape=jax.ShapeDtypeStruct((H_B, LANES, 2 * LANES), F32),
        compiler_params=_cparams("parallel"),
        name="t5_bias",
    )(t5_table)


def _attn_kernel(*refs, mode, lam_init):
    if mode == "mla":
        q_ref, k_ref, v_ref, o_ref, m_ref, acc_ref, s_ref, p_ref, mb_ref = refs
    elif mode == "fox":
        q_ref, k_ref, v_ref, nf_ref, o_ref, m_ref, acc_ref, s_ref, p_ref, mb_ref = refs
    else:
        (q_ref, k_ref, v_ref, bias_ref, lq1_ref, lk1_ref, lq2_ref, lk2_ref, subg_ref,
         o_ref, m_ref, acc_ref, s_ref, p_ref, mb_ref) = refs
    t = o_ref.shape[1]
    i = pl.program_id(2)
    lane = lax.broadcasted_iota(jnp.int32, (t, LANES), 1)
    low = lane < LANES // 2
    if mode == "mla":
        qs = (q_ref[0, 0], q_ref[0, 1])
    else:
        q = q_ref[0, 0]
        zero = jnp.zeros_like(q)
        qs = (jnp.where(low, q, zero), jnp.where(low, zero, q))

    m_ref[...] = jnp.full(m_ref.shape, NEG_INF, F32)
    acc_ref[...] = jnp.zeros(acc_ref.shape, F32)

    def logits(s, j, diag=False):
        k = k_ref[0, s, j] if mode == "mla" else k_ref[0, 0, j]
        if diag:
            h = t // 2
            top = _dot_nt(qs[s][:h], k[:h])
            bottom = _dot_nt(qs[s][h:], k)
            if mode == "fox":
                top = top + nf_ref[0, s, j][:, :h]
                bottom = bottom + nf_ref[0, s, j]
            s_ref[s, :h, :h] = top
            s_ref[s, h:] = bottom
            return
        sc = _dot_nt(qs[s], k)
        if mode == "fox":
            sc = sc + nf_ref[0, s, j]
        s_ref[s] = sc

    def softmax_pv(s, j, diag):
        if mode == "diff":
            if diag:
                s_ref[s, :LANES, :LANES] += bias_ref[0, :, LANES:]
                for r in range(1, t // LANES):
                    s_ref[s, r * LANES:(r + 1) * LANES, (r - 1) * LANES:(r + 1) * LANES] += bias_ref[0]
            else:
                is_prev = (j == i - 1).astype(F32)
                s_ref[s, :LANES, t - LANES:] += bias_ref[0, :, :LANES] * is_prev

        def visible_cols(r):
            if not diag:
                return t
            last_row = (r + 1) * STRIP - 1
            last_col = last_row if mode == "fox" else (last_row // CHUNK + 1) * CHUNK - 1
            return (last_col // LANES + 1) * LANES

        m_all = m_ref[s]
        m_parts, a_parts = [], []
        for r in range(t // STRIP):
            rows = slice(r * STRIP, (r + 1) * STRIP)
            ncol = visible_cols(r)
            if diag:
                row = lax.broadcasted_iota(jnp.int32, (STRIP, LANES), 0) + r * STRIP
                col = lax.broadcasted_iota(jnp.int32, (STRIP, LANES), 1) + (ncol - LANES)
                keep = (col <= row) if mode == "fox" else (col // CHUNK) <= (row // CHUNK)
                edge = jnp.where(keep, s_ref[s, rows, ncol - LANES:ncol], NEG_INF)
                s_ref[s, rows, ncol - LANES:ncol] = edge
            blocks = [s_ref[s, rows, c * LANES:(c + 1) * LANES] for c in range(ncol // LANES)]
            mx = functools.reduce(jnp.maximum, blocks)
            m_prev = m_all[rows]
            m_new = jnp.maximum(m_prev, jnp.max(mx, axis=1, keepdims=True))
            m_parts.append(m_new)
            a_parts.append(jnp.exp2(m_prev - m_new))
            mb_ref[s, rows] = jnp.broadcast_to(m_new, (STRIP, LANES))
        for r in range(t // STRIP):
            rows = slice(r * STRIP, (r + 1) * STRIP)
            ncol = visible_cols(r)
            mb = mb_ref[s, rows]
            for c in range(ncol // LANES):
                cols = slice(c * LANES, (c + 1) * LANES)
                p_ref[s, rows, cols] = jnp.exp2((s_ref[s, rows, cols] - mb).astype(BF16))
            limit = t // 2 if (diag and (r + 1) * STRIP <= t // 2) else t
            if ncol < limit:
                p_ref[s, rows, ncol:limit] = jnp.zeros((STRIP, limit - ncol), BF16)
        m_ref[s] = jnp.concatenate(m_parts, axis=0)
        alpha = jnp.concatenate(a_parts, axis=0)
        if diag:
            h = t // 2
            v = v_ref[0, 0, j]
            acc_ref[s, :h] = alpha[:h] * acc_ref[s, :h] + _dot(p_ref[s, :h, :h], v[:h])
            acc_ref[s, h:] = alpha[h:] * acc_ref[s, h:] + _dot(p_ref[s, h:], v)
        else:
            acc_ref[s] = alpha * acc_ref[s] + _dot(p_ref[s], v_ref[0, 0, j])

    logits(0, 0)

    def far_tile(j):
        logits(1, j)
        softmax_pv(0, j, False)
        logits(0, j + 1)
        softmax_pv(1, j, False)

    odd = i % 2

    @pl.when(odd == 1)
    def _():
        far_tile(0)

    def far_pair(jj, carry):
        far_tile(odd + 2 * jj)
        far_tile(odd + 2 * jj + 1)
        return carry

    lax.fori_loop(0, i // 2, far_pair, 0)
    logits(1, i, diag=True)
    softmax_pv(0, i, True)
    softmax_pv(1, i, True)

    o0 = acc_ref[0, :, :LANES] / acc_ref[0, :, LANES:LANES + 1]
    o1 = acc_ref[1, :, :LANES] / acc_ref[1, :, LANES:LANES + 1]
    if mode == "diff":
        lam = (jnp.exp(jnp.sum(lq1_ref[...] * lk1_ref[...])) - jnp.exp(jnp.sum(lq2_ref[...] * lk2_ref[...]))
               + lam_init)
        o = _rms_norm(o0 - lam * o1, subg_ref[...]) * (1.0 - lam_init)
    else:
        o = jnp.where(low, o0, o1)
    o_ref[0] = o.astype(o_ref.dtype)


def _attention(mode, q, k, v, extra=(), lam_init=0.0):
    bsz, _, s_len, _ = q.shape
    t = min(ATTN_TILE, s_len)
    nk = s_len // t
    n_grp = v.shape[1]
    per = 2 if mode == "mla" else 1
    k5 = k.reshape(bsz, k.shape[1], nk, t, LANES)
    v5 = v.reshape(bsz, n_grp, nk, t, 2 * LANES)
    in_specs = [
        pl.BlockSpec((1, per, t, LANES), lambda b, g, i: (b, g, i, 0)),
        pl.BlockSpec((1, per, nk, t, LANES), lambda b, g, i: (b, g, 0, 0, 0)),
        pl.BlockSpec((1, 1, nk, t, 2 * LANES), lambda b, g, i: (b, g, 0, 0, 0)),
    ]
    args = [q, k5, v5]
    if mode == "fox":
        (neg_f,) = extra
        args.append(neg_f.reshape(bsz, 2 * n_grp, nk, 1, t))
        in_specs.append(pl.BlockSpec((1, 2, nk, 1, t), lambda b, g, i: (b, g, 0, 0, 0)))
    elif mode == "diff":
        bias, lq1, lk1, lq2, lk2, sub_g = extra
        args += [bias, lq1.reshape(1, -1), lk1.reshape(1, -1), lq2.reshape(1, -1), lk2.reshape(1, -1),
                 sub_g.reshape(1, -1)]
        in_specs.append(pl.BlockSpec((1, LANES, 2 * LANES), lambda b, g, i: (g, 0, 0)))
        in_specs += [_const_spec((1, DH_B))] * 4 + [_const_spec((1, 2 * DH_B))]
    scratch = [pltpu.VMEM((2, t, 1), F32), pltpu.VMEM((2, t, 2 * LANES), F32),
               pltpu.VMEM((2, t, t), F32), pltpu.VMEM((2, t, t), BF16), pltpu.VMEM((2, t, LANES), F32)]
    return pl.pallas_call(
        functools.partial(_attn_kernel, mode=mode, lam_init=lam_init),
        grid=(bsz, n_grp, s_len // t),
        in_specs=in_specs,
        out_specs=pl.BlockSpec((1, t, LANES), lambda b, g, i: (b, i, g)),
        out_shape=jax.ShapeDtypeStruct((bsz, s_len, n_grp * LANES), BF16),
        scratch_shapes=scratch,
        compiler_params=_cparams("parallel", "parallel", "arbitrary"),
        name="attn_" + mode,
    )(*args)


def _band_bias_kernel(e_ref, o_ref):
    rows, win = o_ref.shape[1], o_ref.shape[2]
    ext = jnp.broadcast_to(e_ref[0], (rows, e_ref.shape[-1]))
    o_ref[0] = pltpu.roll(ext, win + 1, 1, stride=1, stride_axis=0)[:, :win] * LOG2E


def _band_bias(rel_table):
    ext_len = BAND_WIN + BAND_GROUP
    flipped = rel_table[:, ::-1]
    left = BAND_LEAD - REL_CLIP + BAND_GROUP - 1
    ext = jnp.pad(flipped, ((0, 0), (left, ext_len - left - flipped.shape[1])), mode="edge")
    return pl.pallas_call(
        _band_bias_kernel,
        grid=(H_D,),
        in_specs=[pl.BlockSpec((1, 1, ext_len), lambda h: (h, 0, 0))],
        out_specs=pl.BlockSpec((1, BAND_GROUP, BAND_WIN), lambda h: (h, 0, 0)),
        out_shape=jax.ShapeDtypeStruct((H_D, BAND_GROUP, BAND_WIN), F32),
        compiler_params=_cparams("parallel"),
        name="band_bias",
    )(ext.reshape(H_D, 1, ext_len))


def _band_mask_kernel(o_ref):
    row = lax.broadcasted_iota(jnp.int32, o_ref.shape, 0) % BAND_GROUP
    col = lax.broadcasted_iota(jnp.int32, o_ref.shape, 1)
    first = (row // CHUNK) * CHUNK
    in_band = (col >= first) & (col < first + BAND_LEAD + CHUNK)
    o_ref[...] = jnp.where(in_band, col, -BAND_WIN * 2)


def _band_mask():
    shape = (2 * BAND_GROUP, BAND_WIN)
    return pl.pallas_call(
        _band_mask_kernel,
        out_specs=pl.BlockSpec(shape, lambda: (0, 0)),
        out_shape=jax.ShapeDtypeStruct(shape, jnp.int32),
        compiler_params=pltpu.CompilerParams(vmem_limit_bytes=VMEM_LIMIT),
        name="band_mask",
    )()


def _band_kernel(q_ref, kp_ref, kc_ref, vp_ref, vc_ref, bias_ref, vis_ref, o_ref):
    t = o_ref.shape[1]
    i = pl.program_id(2)
    lane = lax.broadcasted_iota(jnp.int32, (BAND_GROUP, LANES), 1)
    low = lane < LANES // 2
    kw = jnp.concatenate([kp_ref[0, 0], kc_ref[0, 0]], axis=0)
    vw = jnp.concatenate([vp_ref[0, 0], vc_ref[0, 0]], axis=0)
    for r in range(t // BAND_GROUP):
        q = q_ref[0, 0, r * BAND_GROUP:(r + 1) * BAND_GROUP]
        zero = jnp.zeros_like(q)
        q2 = jnp.concatenate([jnp.where(low, q, zero), jnp.where(low, zero, q)], axis=0)
        w0 = t - BAND_LEAD + r * BAND_GROUP
        keep = vis_ref[...] >= jnp.maximum(BAND_LEAD - r * BAND_GROUP - i * t, 0)
        sc = _dot_nt(q2, kw[w0:w0 + BAND_WIN]) + bias_ref[0]
        sc = jnp.where(keep, sc, NEG_INF)
        p = jnp.exp2((sc - jnp.max(sc, axis=1, keepdims=True)).astype(BF16))
        o2 = _dot(p, vw[w0:w0 + BAND_WIN])
        o2 = o2[:, :LANES] / o2[:, LANES:LANES + 1]
        o = jnp.where(low, o2[:BAND_GROUP], o2[BAND_GROUP:])
        o_ref[0, r * BAND_GROUP:(r + 1) * BAND_GROUP] = o.astype(o_ref.dtype)


def _band_attention(q, k, v, bias):
    bsz, n_grp, s_len, _ = q.shape
    t = min(BAND_TILE, s_len)
    assert t >= BAND_LEAD
    cur = lambda b, g, i: (b, g, i, 0)
    prev = lambda b, g, i: (b, g, jnp.maximum(i - 1, 0), 0)
    blk = (1, 1, t, LANES)
    vblk = (1, 1, t, 2 * LANES)
    return pl.pallas_call(
        _band_kernel,
        grid=(bsz, n_grp, s_len // t),
        in_specs=[pl.BlockSpec(blk, cur), pl.BlockSpec(blk, prev), pl.BlockSpec(blk, cur),
                  pl.BlockSpec(vblk, prev), pl.BlockSpec(vblk, cur),
                  pl.BlockSpec((1, 2 * BAND_GROUP, BAND_WIN), lambda b, g, i: (g, 0, 0)),
                  _const_spec((2 * BAND_GROUP, BAND_WIN))],
        out_specs=pl.BlockSpec((1, t, LANES), lambda b, g, i: (b, i, g)),
        out_shape=jax.ShapeDtypeStruct((bsz, s_len, n_grp * LANES), BF16),
        compiler_params=_cparams("parallel", "parallel", "parallel"),
        name="attn_band",
    )(q, k, k, v, v, bias.reshape(n_grp, 2 * BAND_GROUP, BAND_WIN), _band_mask())


def _mix_ffn_kernel(oa_ref, ob_ref, wo_ref, x_ref, mod1_ref, g1_ref, b1_ref,
                    mod_ref, wg_ref, wv_ref, cw_ref, cb_ref, wd_ref, g_ref, b_ref, o_ref, carry_ref):
    d = x_ref.shape[-1]
    tm = x_ref.shape[1]
    half = oa_ref.shape[-1]
    y = _dot(oa_ref[0], wo_ref[:half]) + _dot(ob_ref[0], wo_ref[half:])
    x = _layer_norm(DEEPNORM_ALPHA * x_ref[0] + mod1_ref[0][:, 2 * d:] * y, g1_ref[...], b1_ref[...])
    u, gate = _modulate(x, mod_ref[0], d)
    u = u.astype(BF16)
    d_ff = wg_ref.shape[1]
    n_tiles = d_ff // MXU_COLS
    edges = [(c * n_tiles // FFN_COL_CHUNKS) * MXU_COLS for c in range(FFN_COL_CHUNKS)] + [d_ff]
    seq_start = pl.program_id(1) == 0
    y = None
    for c in range(FFN_COL_CHUNKS):
        cols = slice(edges[c], edges[c + 1])
        row = lax.broadcasted_iota(jnp.int32, (tm, edges[c + 1] - edges[c]), 0)
        g = _dot(u, wg_ref[:, cols])
        val = _dot(u, wv_ref[:, cols])
        prev = jnp.where(seq_start, 0.0, carry_ref[:, cols])
        g1 = jnp.where(row == 0, prev[7:8], pltpu.roll(g, 1, 0))
        g2 = jnp.where(row == 0, prev[6:7], jnp.where(row == 1, prev[7:8], pltpu.roll(g, 2, 0)))
        carry_ref[:, cols] = g[tm - 8:]
        cw = cw_ref[:, cols]
        gc = cw[0:1] * g2 + cw[1:2] * g1 + cw[2:3] * g + cb_ref[:, cols]
        hmid = (gc * jax.nn.sigmoid(gc) * val).astype(BF16)
        yc = _dot(hmid, wd_ref[cols, :])
        y = yc if y is None else y + yc
    o_ref[0] = _layer_norm(DEEPNORM_ALPHA * x + gate * y, g_ref[...], b_ref[...])


def _mix_ffn(oa, ob, w_out, x, mod_mix, ln_g_mix, ln_b_mix,
             mod, w_gate, w_val, conv_w, conv_b, w_down, ln_g, ln_b):
    bsz, s_len, d = x.shape
    d_ff = w_gate.shape[1]
    tm = min(FFN_ROW_TILE, s_len)
    row = lambda b, t: (b, t, 0)
    per_batch = pl.BlockSpec((1, 1, 3 * d), lambda b, t: (b, 0, 0))
    once = lambda shape: pl.BlockSpec(shape, lambda b, t: (0,) * len(shape), pipeline_mode=pl.Buffered(1))
    return pl.pallas_call(
        _mix_ffn_kernel,
        grid=(bsz, s_len // tm),
        in_specs=[
            pl.BlockSpec((1, tm, oa.shape[-1]), row), pl.BlockSpec((1, tm, ob.shape[-1]), row),
            once(w_out.shape),
            pl.BlockSpec((1, tm, d), row),
            per_batch, _const_spec((1, d)), _const_spec((1, d)),
            per_batch,
            once((d, d_ff)), once((d, d_ff)),
            _const_spec((CONV_W, d_ff)), _const_spec((1, d_ff)),
            once((d_ff, d)),
            _const_spec((1, d)), _const_spec((1, d)),
        ],
        out_specs=pl.BlockSpec((1, tm, d), row),
        out_shape=jax.ShapeDtypeStruct((bsz, s_len, d), F32),
        scratch_shapes=[pltpu.VMEM((8, d_ff), F32)],
        compiler_params=_cparams("parallel", "arbitrary"),
        name="mix_ffn",
    )(oa, ob, w_out.astype(BF16), x, mod_mix, ln_g_mix.reshape(1, d), ln_b_mix.reshape(1, d),
      mod, w_gate.astype(BF16), w_val.astype(BF16), conv_w, conv_b.reshape(1, d_ff),
      w_down.astype(BF16), ln_g.reshape(1, d), ln_b.reshape(1, d))


def kernel(x, c, ada_w, ada_b, ln_g, ln_b, t5_table, ab_w_in, mla_q_norm, mla_w_uq, mla_kv_norm, mla_w_ukv, diff_lq1, diff_lk1, diff_lq2, diff_lk2, diff_sub_g, ab_w_out, cd_w_in, fox_b_f, chunk_rel_table, cd_w_out, ffn_w_gate, ffn_w_val, ffn_conv_w, ffn_conv_b, ffn_w_down):
    bsz, s_len, d = x.shape
    depth = ada_w.shape[0]
    mods = _adaln(c, ada_w, ada_b).reshape(depth, 2, bsz, 1, 3 * d)
    for i in range(depth):
        mod = mods[i, 0]
        if i % 2 == 0:
            e = i // 2
            qa, ka, va, qb, kb, vb = _proj_ab(x, mod, ab_w_in[e], mla_q_norm[e], mla_w_uq[e],
                                              mla_kv_norm[e], mla_w_ukv[e])
            o_first = _attention("mla", qa, ka, va)
            lam_init = 0.8 - 0.6 * math.exp(-0.3 * i)
            o_second = _attention("diff", qb, kb, vb,
                                  extra=(_t5_bias(t5_table), diff_lq1[e], diff_lk1[e], diff_lq2[e],
                                         diff_lk2[e], diff_sub_g[e]),
                                  lam_init=lam_init)
            w_out = ab_w_out[e]
        else:
            o = i // 2
            qc, kc, vc, qd, kd, vd, fl_t = _proj_cd(x, mod, cd_w_in[o])
            neg_f = _fox_scan(fl_t, fox_b_f[o])
            o_first = _attention("fox", qc, kc, vc, extra=(neg_f,))
            o_second = _band_attention(qd, kd, vd, _band_bias(chunk_rel_table[o]))
            w_out = cd_w_out[o]
        x = _mix_ffn(o_first, o_second, w_out, x, mod, ln_g[i, 0], ln_b[i, 0],
                     mods[i, 1], ffn_w_gate[i], ffn_w_val[i], ffn_conv_w[i], ffn_conv_b[i], ffn_w_down[i],
                     ln_g[i, 1], ln_b[i, 1])
    return x
```

```python
import functools
import math

import jax
import jax.numpy as jnp
from jax import lax
from jax.experimental import pallas as pl
from jax.experimental.pallas import tpu as pltpu

DEPTH = 2
CHUNK = 64
H_A, Q_LORA, KV_LORA, NOPE_DIM, ROPE_DIM, V_DIM_A = 8, 256, 128, 64, 32, 64
ROPE_BASE = 10000.0
H_B, DH_B, T5_BUCKETS, T5_MAX_DIST = 4, 64, 32, 128
H_C, DH_C = 8, 64
H_D, DH_D, BAND_CHUNKS, REL_CLIP = 8, 64, 8, 128
CONV_W = 3
EPS_LN = 1e-5
EPS_RMS = 1e-6
DEEPNORM_ALPHA = (2 * DEPTH) ** 0.25
NEG_INF = -1e30
LOG2E = math.log2(math.e)

MXU_COLS = 256
LANES = 128
ATTN_TILE = 1024
STRIP = 64
BAND_TILE = 1024
ROW_TILE = 512
FFN_ROW_TILE = 512
FFN_COL_CHUNKS = 2
BAND_GROUP = 2 * CHUNK
BAND_LEAD = BAND_CHUNKS * CHUNK
BAND_WIN = BAND_LEAD + BAND_GROUP
VMEM_LIMIT = 56 * 1024 * 1024

BF16 = jnp.bfloat16
F32 = jnp.float32


def _cparams(*sem):
    return pltpu.CompilerParams(dimension_semantics=sem, vmem_limit_bytes=VMEM_LIMIT)


def _dot(a, b):
    return jnp.dot(a, b, preferred_element_type=F32)


def _dot_nt(a, b):
    return lax.dot_general(a, b, (((1,), (1,)), ((), ())), preferred_element_type=F32)


def _layer_norm(z, g, b):
    mu = jnp.mean(z, axis=-1, keepdims=True)
    zc = z - mu
    var = jnp.mean(zc * zc, axis=-1, keepdims=True)
    return zc * lax.rsqrt(var + EPS_LN) * g + b


def _rms_norm(z, g):
    ms = jnp.mean(z * z, axis=-1, keepdims=True)
    return z * lax.rsqrt(ms + EPS_RMS) * g


def _ones_column(rows):
    lane = lax.broadcasted_iota(jnp.int32, (rows, LANES), 1)
    return jnp.where(lane == 0, 1.0, 0.0).astype(BF16)


def _const_spec(shape):
    return pl.BlockSpec(shape, lambda *_: (0,) * len(shape))


def _adaln_kernel(c_ref, w_ref, b_ref, o_ref):
    c = c_ref[...]
    cond = c * jax.nn.sigmoid(c)
    y = jnp.dot(cond, w_ref[0], preferred_element_type=F32, precision=lax.Precision.HIGHEST)
    col = lax.broadcasted_iota(jnp.int32, y.shape, 1)
    o_ref[0] = y + b_ref[0] + jnp.where(col >= c.shape[1], 1.0, 0.0)


def _adaln(c, ada_w, ada_b):
    bsz, d = c.shape
    n_sub = ada_w.shape[0] * ada_w.shape[1]
    w = ada_w.reshape(n_sub, d, 3 * d)
    b = ada_b.reshape(n_sub, 1, 3 * d)
    return pl.pallas_call(
        _adaln_kernel,
        grid=(n_sub,),
        in_specs=[
            pl.BlockSpec((bsz, d), lambda l: (0, 0)),
            pl.BlockSpec((1, d, 3 * d), lambda l: (l, 0, 0)),
            pl.BlockSpec((1, 1, 3 * d), lambda l: (l, 0, 0)),
        ],
        out_specs=pl.BlockSpec((1, bsz, 3 * d), lambda l: (l, 0, 0)),
        out_shape=jax.ShapeDtypeStruct((n_sub, bsz, 3 * d), F32),
        compiler_params=_cparams("arbitrary"),
        name="adaln",
    )(c, w, b)


def _modulate(x, mod, d):
    return x * mod[:, d:2 * d] + mod[:, :d], mod[:, 2 * d:]


def _proj_ab_kernel(x_ref, mod_ref, w_in_ref, qn_ref, kvn_ref, wq_ref, wk_ref, wv_ref,
                    cq_ref, sq_ref, ck_ref, sk_ref,
                    qa_ref, ka_ref, va_ref, qb_ref, kb_ref, vb_ref):
    d = x_ref.shape[-1]
    u, _ = _modulate(x_ref[0], mod_ref[0], d)
    h = _dot(u.astype(BF16), w_in_ref[...])
    o = 0
    cq = h[:, o:o + Q_LORA]; o += Q_LORA
    ckv = h[:, o:o + KV_LORA]; o += KV_LORA
    kr_a = h[:, o:o + LANES]; o += LANES
    kr_b = h[:, o:o + LANES]; o += LANES
    wb = H_B * LANES
    qb = h[:, o:o + wb]; o += wb
    kb = h[:, o:o + wb]; o += wb
    vb = h[:, o:o + wb]

    nq = _rms_norm(cq, qn_ref[...]).astype(BF16)
    q12 = _dot(nq, wq_ref[...])
    cq_t, sq_t = cq_ref[...], sq_ref[...]
    for hh in range(H_A):
        a = q12[:, hh * LANES:(hh + 1) * LANES]
        b = q12[:, (H_A + hh) * LANES:(H_A + hh + 1) * LANES]
        qa_ref[0, hh] = (a * cq_t + b * sq_t).astype(BF16)

    nkv = _rms_norm(ckv, kvn_ref[...]).astype(BF16)
    kn = _dot(nkv, wk_ref[...])
    vv = _dot(nkv, wv_ref[...])
    k_rope = kr_a * ck_ref[...] + kr_b * sk_ref[...]
    for hh in range(H_A):
        ka_ref[0, hh] = (kn[:, hh * LANES:(hh + 1) * LANES] + k_rope).astype(BF16)
    ones_col = _ones_column(vv.shape[0])
    for p in range(H_A // 2):
        va_ref[0, p, :, :LANES] = vv[:, p * LANES:(p + 1) * LANES].astype(BF16)
        va_ref[0, p, :, LANES:] = ones_col
    scale_b = DH_B ** -0.5 * LOG2E
    for hh in range(H_B):
        sl = slice(hh * LANES, (hh + 1) * LANES)
        qb_ref[0, hh] = (qb[:, sl] * scale_b).astype(BF16)
        kb_ref[0, hh] = kb[:, sl].astype(BF16)
        vb_ref[0, hh, :, :LANES] = vb[:, sl].astype(BF16)
        vb_ref[0, hh, :, LANES:] = ones_col


def _rope_tables(s_len):
    half = ROPE_DIM // 2
    inv = jnp.power(ROPE_BASE, -jnp.arange(half, dtype=F32) / half)
    ang = jnp.arange(s_len, dtype=F32)[:, None] * inv[None, :]
    cos, sin = jnp.cos(ang), jnp.sin(ang)
    zeros_pad = jnp.zeros((s_len, LANES - NOPE_DIM - ROPE_DIM), F32)
    c_rope = jnp.concatenate([cos, cos], axis=1)
    s_rope = jnp.concatenate([-sin, sin], axis=1)
    cq = jnp.concatenate([jnp.ones((s_len, NOPE_DIM), F32), c_rope, zeros_pad], axis=1)
    ck = jnp.concatenate([jnp.zeros((s_len, NOPE_DIM), F32), c_rope, zeros_pad], axis=1)
    sk = jnp.concatenate([jnp.zeros((s_len, NOPE_DIM), F32), s_rope, zeros_pad], axis=1)
    scale_a = (NOPE_DIM + ROPE_DIM) ** -0.5 * LOG2E
    return cq * scale_a, sk * scale_a, ck, sk


def _swap_halves(w):
    half = w.shape[-1] // 2
    return jnp.concatenate([w[..., half:], w[..., :half]], axis=-1)


def _proj_ab_weights(w_in, w_uq, w_ukv):
    d = w_in.shape[0]
    wb = H_B * 2 * DH_B
    o = 0
    w_cq = w_in[:, o:o + Q_LORA]; o += Q_LORA
    w_ckv = w_in[:, o:o + KV_LORA]; o += KV_LORA
    w_kr = w_in[:, o:o + ROPE_DIM]; o += ROPE_DIM
    w_rest = w_in[:, o:o + 3 * wb]
    lead = jnp.zeros((d, NOPE_DIM), F32)
    tail = jnp.zeros((d, LANES - NOPE_DIM - ROPE_DIM), F32)
    w_kr_a = jnp.concatenate([lead, w_kr, tail], axis=1)
    w_kr_b = jnp.concatenate([lead, _swap_halves(w_kr), tail], axis=1)
    w_in_aug = jnp.concatenate([w_cq, w_ckv, w_kr_a, w_kr_b, w_rest], axis=1).astype(BF16)

    wq = w_uq.reshape(Q_LORA, H_A, NOPE_DIM + ROPE_DIM)
    wq_nope, wq_rope = wq[..., :NOPE_DIM], wq[..., NOPE_DIM:]
    zpad = jnp.zeros((Q_LORA, H_A, LANES - NOPE_DIM - ROPE_DIM), F32)
    wq1 = jnp.concatenate([wq_nope, wq_rope, zpad], axis=-1).reshape(Q_LORA, H_A * LANES)
    wq2 = jnp.concatenate([jnp.zeros_like(wq_nope), _swap_halves(wq_rope), zpad], axis=-1)
    wq12 = jnp.concatenate([wq1, wq2.reshape(Q_LORA, H_A * LANES)], axis=1).astype(BF16)

    wkv = w_ukv.reshape(KV_LORA, H_A, NOPE_DIM + V_DIM_A)
    wk = jnp.concatenate([wkv[..., :NOPE_DIM], jnp.zeros((KV_LORA, H_A, LANES - NOPE_DIM), F32)], axis=-1)
    wk = wk.reshape(KV_LORA, H_A * LANES).astype(BF16)
    wv = wkv[..., NOPE_DIM:].reshape(KV_LORA, H_A * V_DIM_A).astype(BF16)
    return w_in_aug, wq12, wk, wv


def _proj_ab(x, mod, w_in, q_norm, w_uq, kv_norm, w_ukv):
    bsz, s_len, d = x.shape
    tm = min(ROW_TILE, s_len)
    w_in_aug, wq12, wk, wv = _proj_ab_weights(w_in, w_uq, w_ukv)
    tabs = _rope_tables(s_len)
    row = lambda b, t: (b, t, 0)
    head = lambda b, t: (b, 0, t, 0)
    tab_spec = pl.BlockSpec((tm, LANES), lambda b, t: (t, 0))
    out = lambda h, w=LANES: jax.ShapeDtypeStruct((bsz, h, s_len, w), BF16)
    out_spec = lambda h, w=LANES: pl.BlockSpec((1, h, tm, w), head)
    return pl.pallas_call(
        _proj_ab_kernel,
        grid=(bsz, s_len // tm),
        in_specs=[
            pl.BlockSpec((1, tm, d), row),
            pl.BlockSpec((1, 1, 3 * d), lambda b, t: (b, 0, 0)),
            _const_spec(w_in_aug.shape),
            _const_spec((1, Q_LORA)), _const_spec((1, KV_LORA)),
            _const_spec(wq12.shape), _const_spec(wk.shape), _const_spec(wv.shape),
            tab_spec, tab_spec, tab_spec, tab_spec,
        ],
        out_specs=[out_spec(H_A), out_spec(H_A), out_spec(H_A // 2, 2 * LANES),
                   out_spec(H_B), out_spec(H_B), out_spec(H_B, 2 * LANES)],
        out_shape=[out(H_A), out(H_A), out(H_A // 2, 2 * LANES), out(H_B), out(H_B), out(H_B, 2 * LANES)],
        compiler_params=_cparams("parallel", "parallel"),
        name="proj_ab",
    )(x, mod, w_in_aug, q_norm.reshape(1, -1), kv_norm.reshape(1, -1), wq12, wk, wv, *tabs)


def _proj_cd_kernel(x_ref, mod_ref, w_ref, wf_ref, qc_ref, kc_ref, vc_ref, qd_ref, kd_ref, vd_ref, fl_ref):
    d = x_ref.shape[-1]
    u, _ = _modulate(x_ref[0], mod_ref[0], d)
    u = u.astype(BF16)
    h = _dot(u, w_ref[...])
    fl_ref[0] = _dot_nt(wf_ref[...], u)
    scale = DH_C ** -0.5 * LOG2E
    n_pair = H_C // 2
    for k, (ref, sc) in enumerate(((qc_ref, scale), (kc_ref, None), (vc_ref, None),
                                   (qd_ref, scale), (kd_ref, None), (vd_ref, None))):
        for p in range(n_pair):
            blk = h[:, (k * n_pair + p) * LANES:(k * n_pair + p + 1) * LANES]
            if sc is not None:
                blk = blk * sc
            ref[0, p, :, :LANES] = blk.astype(BF16)
            if ref is vc_ref or ref is vd_ref:
                ref[0, p, :, LANES:] = _ones_column(blk.shape[0])


def _proj_cd(x, mod, w_in):
    bsz, s_len, d = x.shape
    tm = min(ROW_TILE, s_len)
    wc, wd = H_C * DH_C, H_D * DH_D
    o = 3 * wc
    w_main = jnp.concatenate([w_in[:, :o], w_in[:, o + H_C:]], axis=1).astype(BF16)
    w_f = w_in[:, o:o + H_C].T.astype(BF16)
    n_pair = H_C // 2
    head = lambda b, t: (b, 0, t, 0)
    out = lambda w: jax.ShapeDtypeStruct((bsz, n_pair, s_len, w), BF16)
    out_spec = lambda w: pl.BlockSpec((1, n_pair, tm, w), head)
    widths = (LANES, LANES, 2 * LANES, LANES, LANES, 2 * LANES)
    return pl.pallas_call(
        _proj_cd_kernel,
        grid=(bsz, s_len // tm),
        in_specs=[
            pl.BlockSpec((1, tm, d), lambda b, t: (b, t, 0)),
            pl.BlockSpec((1, 1, 3 * d), lambda b, t: (b, 0, 0)),
            _const_spec(w_main.shape), _const_spec(w_f.shape),
        ],
        out_specs=[out_spec(w) for w in widths] + [pl.BlockSpec((1, H_C, tm), lambda b, t: (b, 0, t))],
        out_shape=[out(w) for w in widths] + [jax.ShapeDtypeStruct((bsz, H_C, s_len), F32)],
        compiler_params=_cparams("parallel", "parallel"),
        name="proj_cd",
    )(x, mod, w_main, w_f)


def _fox_scan_kernel(fl_ref, bf_ref, o_ref):
    z = fl_ref[0] + bf_ref[...]
    x = jnp.minimum(z, 0.0) - jnp.log1p(jnp.exp(-jnp.abs(z)))
    s_len = x.shape[-1]
    pos = lax.broadcasted_iota(jnp.int32, x.shape, 1)
    sh = 1
    while sh < s_len:
        x = x + jnp.where(pos >= sh, pltpu.roll(x, sh, 1), 0.0)
        sh *= 2
    o_ref[0] = -x * LOG2E


def _fox_scan(fl_t, b_f):
    bsz, h, s_len = fl_t.shape
    return pl.pallas_call(
        _fox_scan_kernel,
        grid=(bsz,),
        in_specs=[pl.BlockSpec((1, h, s_len), lambda b: (b, 0, 0)), _const_spec((h, 1))],
        out_specs=pl.BlockSpec((1, h, s_len), lambda b: (b, 0, 0)),
        out_shape=jax.ShapeDtypeStruct((bsz, h, s_len), F32),
        compiler_params=_cparams("parallel"),
        name="fox_scan",
    )(fl_t, b_f.reshape(h, 1))


def _t5_bias_kernel(tab_ref, o_ref):
    h = pl.program_id(0)
    half = T5_BUCKETS // 2
    max_exact = half // 2
    shape = (LANES, 2 * LANES)
    qq = lax.broadcasted_iota(jnp.int32, shape, 0)
    kk = lax.broadcasted_iota(jnp.int32, shape, 1)
    rel = kk - LANES - qq
    n = jnp.abs(rel)
    large = max_exact + (jnp.log(jnp.maximum(n, 1).astype(F32) / max_exact)
                         / math.log(T5_MAX_DIST / max_exact) * (half - max_exact)).astype(jnp.int32)
    large = jnp.minimum(large, half - 1)
    bucket = jnp.where(rel > 0, half, 0) + jnp.where(n < max_exact, n, large)
    val = jnp.zeros(shape, F32)
    for b in range(T5_BUCKETS):
        val = jnp.where(bucket == b, tab_ref[h, b], val)
    o_ref[0] = (val - tab_ref[h, half - 1]) * LOG2E


def _t5_bias(t5_table):
    return pl.pallas_call(
        _t5_bias_kernel,
        grid=(H_B,),
        in_specs=[pl.BlockSpec(memory_space=pltpu.SMEM)],
        out_specs=pl.BlockSpec((1, LANES, 2 * LANES), lambda h: (h, 0, 0)),
        out_shape=jax.ShapeDtypeStruct((H_B, LANES, 2 * LANES), F32),
        compiler_params=_cparams("parallel"),
        name="t5_bias",
    )(t5_table)


def _attn_kernel(*refs, mode, lam_init):
    if mode == "mla":
        q_ref, k_ref, v_ref, o_ref, m_ref, acc_ref, s_ref, p_ref, mb_ref = refs
    elif mode == "fox":
        q_ref, k_ref, v_ref, nf_ref, o_ref, m_ref, acc_ref, s_ref, p_ref, mb_ref = refs
    else:
        (q_ref, k_ref, v_ref, bias_ref, lq1_ref, lk1_ref, lq2_ref, lk2_ref, subg_ref,
         o_ref, m_ref, acc_ref, s_ref, p_ref, mb_ref) = refs
    t = o_ref.shape[1]
    i = pl.program_id(2)
    lane = lax.broadcasted_iota(jnp.int32, (t, LANES), 1)
    low = lane < LANES // 2
    if mode == "mla":
        qs = (q_ref[0, 0], q_ref[0, 1])
    else:
        q = q_ref[0, 0]
        zero = jnp.zeros_like(q)
        qs = (jnp.where(low, q, zero), jnp.where(low, zero, q))

    m_ref[...] = jnp.full(m_ref.shape, NEG_INF, F32)
    acc_ref[...] = jnp.zeros(acc_ref.shape, F32)

    def logits(s, j, diag=False):
        k = k_ref[0, s, j] if mode == "mla" else k_ref[0, 0, j]
        if diag:
            h = t // 2
            top = _dot_nt(qs[s][:h], k[:h])
            bottom = _dot_nt(qs[s][h:], k)
            if mode == "fox":
                top = top + nf_ref[0, s, j][:, :h]
                bottom = bottom + nf_ref[0, s, j]
            s_ref[s, :h, :h] = top
            s_ref[s, h:] = bottom
            return
        sc = _dot_nt(qs[s], k)
        if mode == "fox":
            sc = sc + nf_ref[0, s, j]
        s_ref[s] = sc

    def softmax_pv(s, j, diag):
        if mode == "diff":
            if diag:
                s_ref[s, :LANES, :LANES] += bias_ref[0, :, LANES:]
                for r in range(1, t // LANES):
                    s_ref[s, r * LANES:(r + 1) * LANES, (r - 1) * LANES:(r + 1) * LANES] += bias_ref[0]
            else:
                is_prev = (j == i - 1).astype(F32)
                s_ref[s, :LANES, t - LANES:] += bias_ref[0, :, :LANES] * is_prev

        def visible_cols(r):
            if not diag:
                return t
            last_row = (r + 1) * STRIP - 1
            last_col = last_row if mode == "fox" else (last_row // CHUNK + 1) * CHUNK - 1
            return (last_col // LANES + 1) * LANES

        m_all = m_ref[s]
        m_parts, a_parts = [], []
        for r in range(t // STRIP):
            rows = slice(r * STRIP, (r + 1) * STRIP)
            ncol = visible_cols(r)
            if diag:
                row = lax.broadcasted_iota(jnp.int32, (STRIP, LANES), 0) + r * STRIP
                col = lax.broadcasted_iota(jnp.int32, (STRIP, LANES), 1) + (ncol - LANES)
                keep = (col <= row) if mode == "fox" else (col // CHUNK) <= (row // CHUNK)
                edge = jnp.where(keep, s_ref[s, rows, ncol - LANES:ncol], NEG_INF)
                s_ref[s, rows, ncol - LANES:ncol] = edge
            blocks = [s_ref[s, rows, c * LANES:(c + 1) * LANES] for c in range(ncol // LANES)]
            mx = functools.reduce(jnp.maximum, blocks)
            m_prev = m_all[rows]
            m_new = jnp.maximum(m_prev, jnp.max(mx, axis=1, keepdims=True))
            m_parts.append(m_new)
            a_parts.append(jnp.exp2(m_prev - m_new))
            mb_ref[s, rows] = jnp.broadcast_to(m_new, (STRIP, LANES))
        for r in range(t // STRIP):
            rows = slice(r * STRIP, (r + 1) * STRIP)
            ncol = visible_cols(r)
            mb = mb_ref[s, rows]
            for c in range(ncol // LANES):
                cols = slice(c * LANES, (c + 1) * LANES)
                p_ref[s, rows, cols] = jnp.exp2((s_ref[s, rows, cols] - mb).astype(BF16))
            limit = t // 2 if (diag and (r + 1) * STRIP <= t // 2) else t
            if ncol < limit:
                p_ref[s, rows, ncol:limit] = jnp.zeros((STRIP, limit - ncol), BF16)
        m_ref[s] = jnp.concatenate(m_parts, axis=0)
        alpha = jnp.concatenate(a_parts, axis=0)
        if diag:
            h = t // 2
            v = v_ref[0, 0, j]
            acc_ref[s, :h] = alpha[:h] * acc_ref[s, :h] + _dot(p_ref[s, :h, :h], v[:h])
            acc_ref[s, h:] = alpha[h:] * acc_ref[s, h:] + _dot(p_ref[s, h:], v)
        else:
            acc_ref[s] = alpha * acc_ref[s] + _dot(p_ref[s], v_ref[0, 0, j])

    logits(0, 0)

    def far_tile(j, next_is_diag=False):
        logits(1, j)
        softmax_pv(0, j, False)
        logits(0, j + 1, diag=next_is_diag)
        softmax_pv(1, j, False)

    def diag_tile():
        logits(1, i, diag=True)
        softmax_pv(0, i, True)
        softmax_pv(1, i, True)

    def far_pair(jj, carry):
        far_tile(2 * jj)
        far_tile(2 * jj + 1)
        return carry

    lax.fori_loop(0, i // 2, far_pair, 0)

    @pl.when(i % 2 == 1)
    def _():
        far_tile(i - 1, next_is_diag=True)
        diag_tile()

    @pl.when(i % 2 == 0)
    def _():
        diag_tile()

    o0 = acc_ref[0, :, :LANES] / acc_ref[0, :, LANES:LANES + 1]
    o1 = acc_ref[1, :, :LANES] / acc_ref[1, :, LANES:LANES + 1]
    if mode == "diff":
        lam = (jnp.exp(jnp.sum(lq1_ref[...] * lk1_ref[...])) - jnp.exp(jnp.sum(lq2_ref[...] * lk2_ref[...]))
               + lam_init)
        o = _rms_norm(o0 - lam * o1, subg_ref[...]) * (1.0 - lam_init)
    else:
        o = jnp.where(low, o0, o1)
    o_ref[0] = o.astype(o_ref.dtype)


def _attention(mode, q, k, v, extra=(), lam_init=0.0):
    bsz, _, s_len, _ = q.shape
    t = min(ATTN_TILE, s_len)
    nk = s_len // t
    n_grp = v.shape[1]
    per = 2 if mode == "mla" else 1
    k5 = k.reshape(bsz, k.shape[1], nk, t, LANES)
    v5 = v.reshape(bsz, n_grp, nk, t, 2 * LANES)
    in_specs = [
        pl.BlockSpec((1, per, t, LANES), lambda b, g, i: (b, g, i, 0)),
        pl.BlockSpec((1, per, nk, t, LANES), lambda b, g, i: (b, g, 0, 0, 0)),
        pl.BlockSpec((1, 1, nk, t, 2 * LANES), lambda b, g, i: (b, g, 0, 0, 0)),
    ]
    args = [q, k5, v5]
    if mode == "fox":
        (neg_f,) = extra
        args.append(neg_f.reshape(bsz, 2 * n_grp, nk, 1, t))
        in_specs.append(pl.BlockSpec((1, 2, nk, 1, t), lambda b, g, i: (b, g, 0, 0, 0)))
    elif mode == "diff":
        bias, lq1, lk1, lq2, lk2, sub_g = extra
        args += [bias, lq1.reshape(1, -1), lk1.reshape(1, -1), lq2.reshape(1, -1), lk2.reshape(1, -1),
                 sub_g.reshape(1, -1)]
        in_specs.append(pl.BlockSpec((1, LANES, 2 * LANES), lambda b, g, i: (g, 0, 0)))
        in_specs += [_const_spec((1, DH_B))] * 4 + [_const_spec((1, 2 * DH_B))]
    scratch = [pltpu.VMEM((2, t, 1), F32), pltpu.VMEM((2, t, 2 * LANES), F32),
               pltpu.VMEM((2, t, t), F32), pltpu.VMEM((2, t, t), BF16), pltpu.VMEM((2, t, LANES), F32)]
    return pl.pallas_call(
        functools.partial(_attn_kernel, mode=mode, lam_init=lam_init),
        grid=(bsz, n_grp, s_len // t),
        in_specs=in_specs,
        out_specs=pl.BlockSpec((1, t, LANES), lambda b, g, i: (b, i, g)),
        out_shape=jax.ShapeDtypeStruct((bsz, s_len, n_grp * LANES), BF16),
        scratch_shapes=scratch,
        compiler_params=_cparams("parallel", "parallel", "arbitrary"),
        name="attn_" + mode,
    )(*args)


def _band_bias_kernel(e_ref, o_ref):
    rows, win = o_ref.shape[1], o_ref.shape[2]
    ext = jnp.broadcast_to(e_ref[0], (rows, e_ref.shape[-1]))
    o_ref[0] = pltpu.roll(ext, win + 1, 1, stride=1, stride_axis=0)[:, :win] * LOG2E


def _band_bias(rel_table):
    ext_len = BAND_WIN + BAND_GROUP
    flipped = rel_table[:, ::-1]
    left = BAND_LEAD - REL_CLIP + BAND_GROUP - 1
    ext = jnp.pad(flipped, ((0, 0), (left, ext_len - left - flipped.shape[1])), mode="edge")
    return pl.pallas_call(
        _band_bias_kernel,
        grid=(H_D,),
        in_specs=[pl.BlockSpec((1, 1, ext_len), lambda h: (h, 0, 0))],
        out_specs=pl.BlockSpec((1, BAND_GROUP, BAND_WIN), lambda h: (h, 0, 0)),
        out_shape=jax.ShapeDtypeStruct((H_D, BAND_GROUP, BAND_WIN), F32),
        compiler_params=_cparams("parallel"),
        name="band_bias",
    )(ext.reshape(H_D, 1, ext_len))


def _band_mask_kernel(o_ref):
    row = lax.broadcasted_iota(jnp.int32, o_ref.shape, 0) % BAND_GROUP
    col = lax.broadcasted_iota(jnp.int32, o_ref.shape, 1)
    first = (row // CHUNK) * CHUNK
    in_band = (col >= first) & (col < first + BAND_LEAD + CHUNK)
    o_ref[...] = jnp.where(in_band, col, -BAND_WIN * 2)


def _band_mask():
    shape = (2 * BAND_GROUP, BAND_WIN)
    return pl.pallas_call(
        _band_mask_kernel,
        out_specs=pl.BlockSpec(shape, lambda: (0, 0)),
        out_shape=jax.ShapeDtypeStruct(shape, jnp.int32),
        compiler_params=pltpu.CompilerParams(vmem_limit_bytes=VMEM_LIMIT),
        name="band_mask",
    )()


def _band_kernel(q_ref, kp_ref, kc_ref, vp_ref, vc_ref, bias_ref, vis_ref, o_ref):
    t = o_ref.shape[1]
    i = pl.program_id(2)
    lane = lax.broadcasted_iota(jnp.int32, (BAND_GROUP, LANES), 1)
    low = lane < LANES // 2
    kw = jnp.concatenate([kp_ref[0, 0], kc_ref[0, 0]], axis=0)
    vw = jnp.concatenate([vp_ref[0, 0], vc_ref[0, 0]], axis=0)
    for r in range(t // BAND_GROUP):
        q = q_ref[0, 0, r * BAND_GROUP:(r + 1) * BAND_GROUP]
        zero = jnp.zeros_like(q)
        q2 = jnp.concatenate([jnp.where(low, q, zero), jnp.where(low, zero, q)], axis=0)
        w0 = t - BAND_LEAD + r * BAND_GROUP
        keep = vis_ref[...] >= jnp.maximum(BAND_LEAD - r * BAND_GROUP - i * t, 0)
        sc = _dot_nt(q2, kw[w0:w0 + BAND_WIN]) + bias_ref[0]
        sc = jnp.where(keep, sc, NEG_INF)
        p = jnp.exp2((sc - jnp.max(sc, axis=1, keepdims=True)).astype(BF16))
        o2 = _dot(p, vw[w0:w0 + BAND_WIN])
        o2 = o2[:, :LANES] / o2[:, LANES:LANES + 1]
        o = jnp.where(low, o2[:BAND_GROUP], o2[BAND_GROUP:])
        o_ref[0, r * BAND_GROUP:(r + 1) * BAND_GROUP] = o.astype(o_ref.dtype)


def _band_attention(q, k, v, bias):
    bsz, n_grp, s_len, _ = q.shape
    t = min(BAND_TILE, s_len)
    assert t >= BAND_LEAD
    cur = lambda b, g, i: (b, g, i, 0)
    prev = lambda b, g, i: (b, g, jnp.maximum(i - 1, 0), 0)
    blk = (1, 1, t, LANES)
    vblk = (1, 1, t, 2 * LANES)
    return pl.pallas_call(
        _band_kernel,
        grid=(bsz, n_grp, s_len // t),
        in_specs=[pl.BlockSpec(blk, cur), pl.BlockSpec(blk, prev), pl.BlockSpec(blk, cur),
                  pl.BlockSpec(vblk, prev), pl.BlockSpec(vblk, cur),
                  pl.BlockSpec((1, 2 * BAND_GROUP, BAND_WIN), lambda b, g, i: (g, 0, 0)),
                  _const_spec((2 * BAND_GROUP, BAND_WIN))],
        out_specs=pl.BlockSpec((1, t, LANES), lambda b, g, i: (b, i, g)),
        out_shape=jax.ShapeDtypeStruct((bsz, s_len, n_grp * LANES), BF16),
        compiler_params=_cparams("parallel", "parallel", "parallel"),
        name="attn_band",
    )(q, k, k, v, v, bias.reshape(n_grp, 2 * BAND_GROUP, BAND_WIN), _band_mask())


def _mix_ffn_kernel(oa_ref, ob_ref, wo_ref, x_ref, mod1_ref, g1_ref, b1_ref,
                    mod_ref, wg_ref, wv_ref, cw_ref, cb_ref, wd_ref, g_ref, b_ref, o_ref, carry_ref):
    d = x_ref.shape[-1]
    tm = x_ref.shape[1]
    half = oa_ref.shape[-1]
    y = _dot(oa_ref[0], wo_ref[:half]) + _dot(ob_ref[0], wo_ref[half:])
    x = _layer_norm(DEEPNORM_ALPHA * x_ref[0] + mod1_ref[0][:, 2 * d:] * y, g1_ref[...], b1_ref[...])
    u, gate = _modulate(x, mod_ref[0], d)
    u = u.astype(BF16)
    d_ff = wg_ref.shape[1]
    n_tiles = d_ff // MXU_COLS
    edges = [(c * n_tiles // FFN_COL_CHUNKS) * MXU_COLS for c in range(FFN_COL_CHUNKS)] + [d_ff]
    seq_start = pl.program_id(1) == 0
    y = None
    for c in range(FFN_COL_CHUNKS):
        cols = slice(edges[c], edges[c + 1])
        row = lax.broadcasted_iota(jnp.int32, (tm, edges[c + 1] - edges[c]), 0)
        g = _dot(u, wg_ref[:, cols])
        val = _dot(u, wv_ref[:, cols])
        prev = jnp.where(seq_start, 0.0, carry_ref[:, cols])
        g1 = jnp.where(row == 0, prev[7:8], pltpu.roll(g, 1, 0))
        g2 = jnp.where(row == 0, prev[6:7], jnp.where(row == 1, prev[7:8], pltpu.roll(g, 2, 0)))
        carry_ref[:, cols] = g[tm - 8:]
        cw = cw_ref[:, cols]
        gc = cw[0:1] * g2 + cw[1:2] * g1 + cw[2:3] * g + cb_ref[:, cols]
        hmid = (gc * jax.nn.sigmoid(gc) * val).astype(BF16)
        yc = _dot(hmid, wd_ref[cols, :])
        y = yc if y is None else y + yc
    o_ref[0] = _layer_norm(DEEPNORM_ALPHA * x + gate * y, g_ref[...], b_ref[...])


def _mix_ffn(oa, ob, w_out, x, mod_mix, ln_g_mix, ln_b_mix,
             mod, w_gate, w_val, conv_w, conv_b, w_down, ln_g, ln_b):
    bsz, s_len, d = x.shape
    d_ff = w_gate.shape[1]
    tm = min(FFN_ROW_TILE, s_len)
    row = lambda b, t: (b, t, 0)
    per_batch = pl.BlockSpec((1, 1, 3 * d), lambda b, t: (b, 0, 0))
    once = lambda shape: pl.BlockSpec(shape, lambda b, t: (0,) * len(shape), pipeline_mode=pl.Buffered(1))
    return pl.pallas_call(
        _mix_ffn_kernel,
        grid=(bsz, s_len // tm),
        in_specs=[
            pl.BlockSpec((1, tm, oa.shape[-1]), row), pl.BlockSpec((1, tm, ob.shape[-1]), row),
            once(w_out.shape),
            pl.BlockSpec((1, tm, d), row),
            per_batch, _const_spec((1, d)), _const_spec((1, d)),
            per_batch,
            once((d, d_ff)), once((d, d_ff)),
            _const_spec((CONV_W, d_ff)), _const_spec((1, d_ff)),
            once((d_ff, d)),
            _const_spec((1, d)), _const_spec((1, d)),
        ],
        out_specs=pl.BlockSpec((1, tm, d), row),
        out_shape=jax.ShapeDtypeStruct((bsz, s_len, d), F32),
        scratch_shapes=[pltpu.VMEM((8, d_ff), F32)],
        compiler_params=_cparams("parallel", "arbitrary"),
        name="mix_ffn",
    )(oa, ob, w_out.astype(BF16), x, mod_mix, ln_g_mix.reshape(1, d), ln_b_mix.reshape(1, d),
      mod, w_gate.astype(BF16), w_val.astype(BF16), conv_w, conv_b.reshape(1, d_ff),
      w_down.astype(BF16), ln_g.reshape(1, d), ln_b.reshape(1, d))


def kernel(x, c, ada_w, ada_b, ln_g, ln_b, t5_table, ab_w_in, mla_q_norm, mla_w_uq, mla_kv_norm, mla_w_ukv, diff_lq1, diff_lk1, diff_lq2, diff_lk2, diff_sub_g, ab_w_out, cd_w_in, fox_b_f, chunk_rel_table, cd_w_out, ffn_w_gate, ffn_w_val, ffn_conv_w, ffn_conv_b, ffn_w_down):
    bsz, s_len, d = x.shape
    depth = ada_w.shape[0]
    mods = _adaln(c, ada_w, ada_b).reshape(depth, 2, bsz, 1, 3 * d)
    for i in range(depth):
        mod = mods[i, 0]
        if i % 2 == 0:
            e = i // 2
            qa, ka, va, qb, kb, vb = _proj_ab(x, mod, ab_w_in[e], mla_q_norm[e], mla_w_uq[e],
                                              mla_kv_norm[e], mla_w_ukv[e])
            o_first = _attention("mla", qa, ka, va)
            lam_init = 0.8 - 0.6 * math.exp(-0.3 * i)
            o_second = _attention("diff", qb, kb, vb,
                                  extra=(_t5_bias(t5_table), diff_lq1[e], diff_lk1[e], diff_lq2[e],
                                         diff_lk2[e], diff_sub_g[e]),
                                  lam_init=lam_init)
            w_out = ab_w_out[e]
        else:
            o = i // 2
            qc, kc, vc, qd, kd, vd, fl_t = _proj_cd(x, mod, cd_w_in[o])
            neg_f = _fox_scan(fl_t, fox_b_f[o])
            o_first = _attention("fox", qc, kc, vc, extra=(neg_f,))
            o_second = _band_attention(qd, kd, vd, _band_bias(chunk_rel_table[o]))
            w_out = cd_w_out[o]
        x = _mix_ffn(o_first, o_second, w_out, x, mod, ln_g[i, 0], ln_b[i, 0],
                     mods[i, 1], ffn_w_gate[i], ffn_w_val[i], ffn_conv_w[i], ffn_conv_b[i], ffn_w_down[i],
                     ln_g[i, 1], ln_b[i, 1])
    return x
```

```python
import functools
import math

import jax
import jax.numpy as jnp
from jax import lax
from jax.experimental import pallas as pl
from jax.experimental.pallas import tpu as pltpu

DEPTH = 2
CHUNK = 64
H_A, Q_LORA, KV_LORA, NOPE_DIM, ROPE_DIM, V_DIM_A = 8, 256, 128, 64, 32, 64
ROPE_BASE = 10000.0
H_B, DH_B, T5_BUCKETS, T5_MAX_DIST = 4, 64, 32, 128
H_C, DH_C = 8, 64
H_D, DH_D, BAND_CHUNKS, REL_CLIP = 8, 64, 8, 128
CONV_W = 3
EPS_LN = 1e-5
EPS_RMS = 1e-6
DEEPNORM_ALPHA = (2 * DEPTH) ** 0.25
NEG_INF = -1e30
LOG2E = math.log2(math.e)

MXU_COLS = 256
LANES = 128
ATTN_TILE = 1024
STRIP = 64
BAND_TILE = 2048
ROW_TILE = 512
FFN_ROW_TILE = 512
FFN_COL_CHUNKS = 2
BAND_GROUP = 2 * CHUNK
BAND_LEAD = BAND_CHUNKS * CHUNK
BAND_WIN = BAND_LEAD + BAND_GROUP
VMEM_LIMIT = 56 * 1024 * 1024

BF16 = jnp.bfloat16
F32 = jnp.float32


def _cparams(*sem):
    return pltpu.CompilerParams(dimension_semantics=sem, vmem_limit_bytes=VMEM_LIMIT)


def _dot(a, b):
    return jnp.dot(a, b, preferred_element_type=F32)


def _dot_nt(a, b):
    return lax.dot_general(a, b, (((1,), (1,)), ((), ())), preferred_element_type=F32)


def _layer_norm(z, g, b):
    mu = jnp.mean(z, axis=-1, keepdims=True)
    zc = z - mu
    var = jnp.mean(zc * zc, axis=-1, keepdims=True)
    return zc * lax.rsqrt(var + EPS_LN) * g + b


def _rms_norm(z, g):
    ms = jnp.mean(z * z, axis=-1, keepdims=True)
    return z * lax.rsqrt(ms + EPS_RMS) * g


def _ones_column(rows):
    lane = lax.broadcasted_iota(jnp.int32, (rows, LANES), 1)
    return jnp.where(lane == 0, 1.0, 0.0).astype(BF16)


def _const_spec(shape):
    return pl.BlockSpec(shape, lambda *_: (0,) * len(shape))


def _adaln_kernel(c_ref, w_ref, b_ref, o_ref):
    c = c_ref[...]
    cond = c * jax.nn.sigmoid(c)
    y = _dot(cond.astype(BF16), w_ref[0].astype(BF16))
    col = lax.broadcasted_iota(jnp.int32, y.shape, 1)
    o_ref[0] = y + b_ref[0] + jnp.where(col >= c.shape[1], 1.0, 0.0)


def _adaln(c, ada_w, ada_b):
    bsz, d = c.shape
    n_sub = ada_w.shape[0] * ada_w.shape[1]
    w = ada_w.reshape(n_sub, d, 3 * d)
    b = ada_b.reshape(n_sub, 1, 3 * d)
    return pl.pallas_call(
        _adaln_kernel,
        grid=(n_sub,),
        in_specs=[
            pl.BlockSpec((bsz, d), lambda l: (0, 0)),
            pl.BlockSpec((1, d, 3 * d), lambda l: (l, 0, 0)),
            pl.BlockSpec((1, 1, 3 * d), lambda l: (l, 0, 0)),
        ],
        out_specs=pl.BlockSpec((1, bsz, 3 * d), lambda l: (l, 0, 0)),
        out_shape=jax.ShapeDtypeStruct((n_sub, bsz, 3 * d), F32),
        compiler_params=_cparams("arbitrary"),
        name="adaln",
    )(c, w, b)


def _modulate(x, mod, d):
    return x * mod[:, d:2 * d] + mod[:, :d], mod[:, 2 * d:]


def _proj_ab_kernel(x_ref, mod_ref, w_in_ref, qn_ref, kvn_ref, wq_ref, wk_ref, wv_ref,
                    cq_ref, sq_ref, ck_ref, sk_ref,
                    qa_ref, ka_ref, va_ref, qb_ref, kb_ref, vb_ref):
    d = x_ref.shape[-1]
    u, _ = _modulate(x_ref[0], mod_ref[0], d)
    h = _dot(u.astype(BF16), w_in_ref[...])
    o = 0
    cq = h[:, o:o + Q_LORA]; o += Q_LORA
    ckv = h[:, o:o + KV_LORA]; o += KV_LORA
    kr_a = h[:, o:o + LANES]; o += LANES
    kr_b = h[:, o:o + LANES]; o += LANES
    wb = H_B * LANES
    qb = h[:, o:o + wb]; o += wb
    kb = h[:, o:o + wb]; o += wb
    vb = h[:, o:o + wb]

    nq = _rms_norm(cq, qn_ref[...]).astype(BF16)
    q12 = _dot(nq, wq_ref[...])
    cq_t, sq_t = cq_ref[...], sq_ref[...]
    for hh in range(H_A):
        a = q12[:, hh * LANES:(hh + 1) * LANES]
        b = q12[:, (H_A + hh) * LANES:(H_A + hh + 1) * LANES]
        qa_ref[0, hh] = (a * cq_t + b * sq_t).astype(BF16)

    nkv = _rms_norm(ckv, kvn_ref[...]).astype(BF16)
    kn = _dot(nkv, wk_ref[...])
    vv = _dot(nkv, wv_ref[...])
    k_rope = kr_a * ck_ref[...] + kr_b * sk_ref[...]
    for hh in range(H_A):
        ka_ref[0, hh] = (kn[:, hh * LANES:(hh + 1) * LANES] + k_rope).astype(BF16)
    ones_col = _ones_column(vv.shape[0])
    for p in range(H_A // 2):
        va_ref[0, p, :, :LANES] = vv[:, p * LANES:(p + 1) * LANES].astype(BF16)
        va_ref[0, p, :, LANES:] = ones_col
    scale_b = DH_B ** -0.5 * LOG2E
    for hh in range(H_B):
        sl = slice(hh * LANES, (hh + 1) * LANES)
        qb_ref[0, hh] = (qb[:, sl] * scale_b).astype(BF16)
        kb_ref[0, hh] = kb[:, sl].astype(BF16)
        vb_ref[0, hh, :, :LANES] = vb[:, sl].astype(BF16)
        vb_ref[0, hh, :, LANES:] = ones_col


def _rope_tables(s_len):
    half = ROPE_DIM // 2
    inv = jnp.power(ROPE_BASE, -jnp.arange(half, dtype=F32) / half)
    ang = jnp.arange(s_len, dtype=F32)[:, None] * inv[None, :]
    cos, sin = jnp.cos(ang), jnp.sin(ang)
    zeros_pad = jnp.zeros((s_len, LANES - NOPE_DIM - ROPE_DIM), F32)
    c_rope = jnp.concatenate([cos, cos], axis=1)
    s_rope = jnp.concatenate([-sin, sin], axis=1)
    cq = jnp.concatenate([jnp.ones((s_len, NOPE_DIM), F32), c_rope, zeros_pad], axis=1)
    ck = jnp.concatenate([jnp.zeros((s_len, NOPE_DIM), F32), c_rope, zeros_pad], axis=1)
    sk = jnp.concatenate([jnp.zeros((s_len, NOPE_DIM), F32), s_rope, zeros_pad], axis=1)
    scale_a = (NOPE_DIM + ROPE_DIM) ** -0.5 * LOG2E
    return cq * scale_a, sk * scale_a, ck, sk


def _swap_halves(w):
    half = w.shape[-1] // 2
    return jnp.concatenate([w[..., half:], w[..., :half]], axis=-1)


def _proj_ab_weights(w_in, w_uq, w_ukv):
    d = w_in.shape[0]
    wb = H_B * 2 * DH_B
    o = 0
    w_cq = w_in[:, o:o + Q_LORA]; o += Q_LORA
    w_ckv = w_in[:, o:o + KV_LORA]; o += KV_LORA
    w_kr = w_in[:, o:o + ROPE_DIM]; o += ROPE_DIM
    w_rest = w_in[:, o:o + 3 * wb]
    lead = jnp.zeros((d, NOPE_DIM), F32)
    tail = jnp.zeros((d, LANES - NOPE_DIM - ROPE_DIM), F32)
    w_kr_a = jnp.concatenate([lead, w_kr, tail], axis=1)
    w_kr_b = jnp.concatenate([lead, _swap_halves(w_kr), tail], axis=1)
    w_in_aug = jnp.concatenate([w_cq, w_ckv, w_kr_a, w_kr_b, w_rest], axis=1).astype(BF16)

    wq = w_uq.reshape(Q_LORA, H_A, NOPE_DIM + ROPE_DIM)
    wq_nope, wq_rope = wq[..., :NOPE_DIM], wq[..., NOPE_DIM:]
    zpad = jnp.zeros((Q_LORA, H_A, LANES - NOPE_DIM - ROPE_DIM), F32)
    wq1 = jnp.concatenate([wq_nope, wq_rope, zpad], axis=-1).reshape(Q_LORA, H_A * LANES)
    wq2 = jnp.concatenate([jnp.zeros_like(wq_nope), _swap_halves(wq_rope), zpad], axis=-1)
    wq12 = jnp.concatenate([wq1, wq2.reshape(Q_LORA, H_A * LANES)], axis=1).astype(BF16)

    wkv = w_ukv.reshape(KV_LORA, H_A, NOPE_DIM + V_DIM_A)
    wk = jnp.concatenate([wkv[..., :NOPE_DIM], jnp.zeros((KV_LORA, H_A, LANES - NOPE_DIM), F32)], axis=-1)
    wk = wk.reshape(KV_LORA, H_A * LANES).astype(BF16)
    wv = wkv[..., NOPE_DIM:].reshape(KV_LORA, H_A * V_DIM_A).astype(BF16)
    return w_in_aug, wq12, wk, wv


def _proj_ab(x, mod, w_in, q_norm, w_uq, kv_norm, w_ukv):
    bsz, s_len, d = x.shape
    tm = min(ROW_TILE, s_len)
    w_in_aug, wq12, wk, wv = _proj_ab_weights(w_in, w_uq, w_ukv)
    tabs = _rope_tables(s_len)
    row = lambda b, t: (b, t, 0)
    head = lambda b, t: (b, 0, t, 0)
    tab_spec = pl.BlockSpec((tm, LANES), lambda b, t: (t, 0))
    out = lambda h, w=LANES: jax.ShapeDtypeStruct((bsz, h, s_len, w), BF16)
    out_spec = lambda h, w=LANES: pl.BlockSpec((1, h, tm, w), head)
    return pl.pallas_call(
        _proj_ab_kernel,
        grid=(bsz, s_len // tm),
        in_specs=[
            pl.BlockSpec((1, tm, d), row),
            pl.BlockSpec((1, 1, 3 * d), lambda b, t: (b, 0, 0)),
            _const_spec(w_in_aug.shape),
            _const_spec((1, Q_LORA)), _const_spec((1, KV_LORA)),
            _const_spec(wq12.shape), _const_spec(wk.shape), _const_spec(wv.shape),
            tab_spec, tab_spec, tab_spec, tab_spec,
        ],
        out_specs=[out_spec(H_A), out_spec(H_A), out_spec(H_A // 2, 2 * LANES),
                   out_spec(H_B), out_spec(H_B), out_spec(H_B, 2 * LANES)],
        out_shape=[out(H_A), out(H_A), out(H_A // 2, 2 * LANES), out(H_B), out(H_B), out(H_B, 2 * LANES)],
        compiler_params=_cparams("parallel", "parallel"),
        name="proj_ab",
    )(x, mod, w_in_aug, q_norm.reshape(1, -1), kv_norm.reshape(1, -1), wq12, wk, wv, *tabs)


def _proj_cd_kernel(x_ref, mod_ref, w_ref, wf_ref, qc_ref, kc_ref, vc_ref, qd_ref, kd_ref, vd_ref, fl_ref):
    d = x_ref.shape[-1]
    u, _ = _modulate(x_ref[0], mod_ref[0], d)
    u = u.astype(BF16)
    h = _dot(u, w_ref[...])
    fl_ref[0] = _dot_nt(wf_ref[...], u)
    scale = DH_C ** -0.5 * LOG2E
    n_pair = H_C // 2
    for k, (ref, sc) in enumerate(((qc_ref, scale), (kc_ref, None), (vc_ref, None),
                                   (qd_ref, scale), (kd_ref, None), (vd_ref, None))):
        for p in range(n_pair):
            blk = h[:, (k * n_pair + p) * LANES:(k * n_pair + p + 1) * LANES]
            if sc is not None:
                blk = blk * sc
            ref[0, p, :, :LANES] = blk.astype(BF16)
            if ref is vc_ref or ref is vd_ref:
                ref[0, p, :, LANES:] = _ones_column(blk.shape[0])


def _proj_cd(x, mod, w_in):
    bsz, s_len, d = x.shape
    tm = min(ROW_TILE, s_len)
    wc, wd = H_C * DH_C, H_D * DH_D
    o = 3 * wc
    w_main = jnp.concatenate([w_in[:, :o], w_in[:, o + H_C:]], axis=1).astype(BF16)
    w_f = w_in[:, o:o + H_C].T.astype(BF16)
    n_pair = H_C // 2
    head = lambda b, t: (b, 0, t, 0)
    out = lambda w: jax.ShapeDtypeStruct((bsz, n_pair, s_len, w), BF16)
    out_spec = lambda w: pl.BlockSpec((1, n_pair, tm, w), head)
    widths = (LANES, LANES, 2 * LANES, LANES, LANES, 2 * LANES)
    return pl.pallas_call(
        _proj_cd_kernel,
        grid=(bsz, s_len // tm),
        in_specs=[
            pl.BlockSpec((1, tm, d), lambda b, t: (b, t, 0)),
            pl.BlockSpec((1, 1, 3 * d), lambda b, t: (b, 0, 0)),
            _const_spec(w_main.shape), _const_spec(w_f.shape),
        ],
        out_specs=[out_spec(w) for w in widths] + [pl.BlockSpec((1, H_C, tm), lambda b, t: (b, 0, t))],
        out_shape=[out(w) for w in widths] + [jax.ShapeDtypeStruct((bsz, H_C, s_len), F32)],
        compiler_params=_cparams("parallel", "parallel"),
        name="proj_cd",
    )(x, mod, w_main, w_f)


def _fox_scan_kernel(fl_ref, bf_ref, o_ref):
    z = fl_ref[0] + bf_ref[...]
    x = jnp.minimum(z, 0.0) - jnp.log1p(jnp.exp(-jnp.abs(z)))
    s_len = x.shape[-1]
    pos = lax.broadcasted_iota(jnp.int32, x.shape, 1)
    sh = 1
    while sh < s_len:
        x = x + jnp.where(pos >= sh, pltpu.roll(x, sh, 1), 0.0)
        sh *= 2
    o_ref[0] = -x * LOG2E


def _fox_scan(fl_t, b_f):
    bsz, h, s_len = fl_t.shape
    return pl.pallas_call(
        _fox_scan_kernel,
        grid=(bsz,),
        in_specs=[pl.BlockSpec((1, h, s_len), lambda b: (b, 0, 0)), _const_spec((h, 1))],
        out_specs=pl.BlockSpec((1, h, s_len), lambda b: (b, 0, 0)),
        out_shape=jax.ShapeDtypeStruct((bsz, h, s_len), F32),
        compiler_params=_cparams("parallel"),
        name="fox_scan",
    )(fl_t, b_f.reshape(h, 1))


def _t5_bias_kernel(tab_ref, o_ref):
    h = pl.program_id(0)
    half = T5_BUCKETS // 2
    max_exact = half // 2
    shape = (LANES, 2 * LANES)
    qq = lax.broadcasted_iota(jnp.int32, shape, 0)
    kk = lax.broadcasted_iota(jnp.int32, shape, 1)
    rel = kk - LANES - qq
    n = jnp.abs(rel)
    large = max_exact + (jnp.log(jnp.maximum(n, 1).astype(F32) / max_exact)
                         / math.log(T5_MAX_DIST / max_exact) * (half - max_exact)).astype(jnp.int32)
    large = jnp.minimum(large, half - 1)
    bucket = jnp.where(rel > 0, half, 0) + jnp.where(n < max_exact, n, large)
    val = jnp.zeros(shape, F32)
    for b in range(T5_BUCKETS):
        val = jnp.where(bucket == b, tab_ref[h, b], val)
    o_ref[0] = (val - tab_ref[h, half - 1]) * LOG2E


def _t5_bias(t5_table):
    return pl.pallas_call(
        _t5_bias_kernel,
        grid=(H_B,),
        in_specs=[pl.BlockSpec(memory_space=pltpu.SMEM)],
        out_specs=pl.BlockSpec((1, LANES, 2 * LANES), lambda h: (h, 0, 0)),
        out_shape=jax.ShapeDtypeStruct((H_B, LANES, 2 * LANES), F32),
        compiler_params=_cparams("parallel"),
        name="t5_bias",
    )(t5_table)


def _attn_kernel(*refs, mode, lam_init):
    if mode == "mla":
        q_ref, k_ref, v_ref, o_ref, m_ref, acc_ref, s_ref, p_ref, mb_ref = refs
    elif mode == "fox":
        q_ref, k_ref, v_ref, nf_ref, o_ref, m_ref, acc_ref, s_ref, p_ref, mb_ref = refs
    else:
        (q_ref, k_ref, v_ref, bias_ref, lq1_ref, lk1_ref, lq2_ref, lk2_ref, subg_ref,
         o_ref, m_ref, acc_ref, s_ref, p_ref, mb_ref) = refs
    t = o_ref.shape[1]
    i = pl.program_id(2)
    lane = lax.broadcasted_iota(jnp.int32, (t, LANES), 1)
    low = lane < LANES // 2
    if mode == "mla":
        qs = (q_ref[0, 0], q_ref[0, 1])
    else:
        q = q_ref[0, 0]
        zero = jnp.zeros_like(q)
        qs = (jnp.where(low, q, zero), jnp.where(low, zero, q))

    m_ref[...] = jnp.full(m_ref.shape, NEG_INF, F32)
    acc_ref[...] = jnp.zeros(acc_ref.shape, F32)

    def logits(s, j, diag=False):
        k = k_ref[0, s, j] if mode == "mla" else k_ref[0, 0, j]
        if diag:
            h = t // 2
            top = _dot_nt(qs[s][:h], k[:h])
            bottom = _dot_nt(qs[s][h:], k)
            if mode == "fox":
                top = top + nf_ref[0, s, j][:, :h]
                bottom = bottom + nf_ref[0, s, j]
            s_ref[s, :h, :h] = top
            s_ref[s, h:] = bottom
            return
        sc = _dot_nt(qs[s], k)
        if mode == "fox":
            sc = sc + nf_ref[0, s, j]
        s_ref[s] = sc

    def softmax_pv(s, j, diag):
        if mode == "diff":
            if diag:
                s_ref[s, :LANES, :LANES] += bias_ref[0, :, LANES:]
                for r in range(1, t // LANES):
                    s_ref[s, r * LANES:(r + 1) * LANES, (r - 1) * LANES:(r + 1) * LANES] += bias_ref[0]
            else:
                is_prev = (j == i - 1).astype(F32)
                s_ref[s, :LANES, t - LANES:] += bias_ref[0, :, :LANES] * is_prev

        def visible_cols(r):
            if not diag:
                return t
            last_row = (r + 1) * STRIP - 1
            last_col = last_row if mode == "fox" else (last_row // CHUNK + 1) * CHUNK - 1
            return (last_col // LANES + 1) * LANES

        m_all = m_ref[s]
        m_parts, a_parts = [], []
        for r in range(t // STRIP):
            rows = slice(r * STRIP, (r + 1) * STRIP)
            ncol = visible_cols(r)
            if diag:
                row = lax.broadcasted_iota(jnp.int32, (STRIP, LANES), 0) + r * STRIP
                col = lax.broadcasted_iota(jnp.int32, (STRIP, LANES), 1) + (ncol - LANES)
                keep = (col <= row) if mode == "fox" else (col // CHUNK) <= (row // CHUNK)
                edge = jnp.where(keep, s_ref[s, rows, ncol - LANES:ncol], NEG_INF)
                s_ref[s, rows, ncol - LANES:ncol] = edge
            blocks = [s_ref[s, rows, c * LANES:(c + 1) * LANES] for c in range(ncol // LANES)]
            mx = functools.reduce(jnp.maximum, blocks)
            m_prev = m_all[rows]
            m_new = jnp.maximum(m_prev, jnp.max(mx, axis=1, keepdims=True))
            m_parts.append(m_new)
            a_parts.append(jnp.exp2(m_prev - m_new))
            mb_ref[s, rows] = jnp.broadcast_to(m_new, (STRIP, LANES))
        for r in range(t // STRIP):
            rows = slice(r * STRIP, (r + 1) * STRIP)
            ncol = visible_cols(r)
            mb = mb_ref[s, rows]
            for c in range(ncol // LANES):
                cols = slice(c * LANES, (c + 1) * LANES)
                p_ref[s, rows, cols] = jnp.exp2((s_ref[s, rows, cols] - mb).astype(BF16))
            limit = t // 2 if (diag and (r + 1) * STRIP <= t // 2) else t
            if ncol < limit:
                p_ref[s, rows, ncol:limit] = jnp.zeros((STRIP, limit - ncol), BF16)
        m_ref[s] = jnp.concatenate(m_parts, axis=0)
        alpha = jnp.concatenate(a_parts, axis=0)
        if diag:
            h = t // 2
            v = v_ref[0, 0, j]
            acc_ref[s, :h] = alpha[:h] * acc_ref[s, :h] + _dot(p_ref[s, :h, :h], v[:h])
            acc_ref[s, h:] = alpha[h:] * acc_ref[s, h:] + _dot(p_ref[s, h:], v)
        else:
            acc_ref[s] = alpha * acc_ref[s] + _dot(p_ref[s], v_ref[0, 0, j])

    logits(0, 0)

    def far_tile(j, next_is_diag=False):
        logits(1, j)
        softmax_pv(0, j, False)
        logits(0, j + 1, diag=next_is_diag)
        softmax_pv(1, j, False)

    def diag_tile():
        logits(1, i, diag=True)
        softmax_pv(0, i, True)
        softmax_pv(1, i, True)

    def far_pair(jj, carry):
        far_tile(2 * jj)
        far_tile(2 * jj + 1)
        return carry

    lax.fori_loop(0, i // 2, far_pair, 0)

    @pl.when(i % 2 == 1)
    def _():
        far_tile(i - 1, next_is_diag=True)
        diag_tile()

    @pl.when(i % 2 == 0)
    def _():
        diag_tile()

    o0 = acc_ref[0, :, :LANES] / acc_ref[0, :, LANES:LANES + 1]
    o1 = acc_ref[1, :, :LANES] / acc_ref[1, :, LANES:LANES + 1]
    if mode == "diff":
        lam = (jnp.exp(jnp.sum(lq1_ref[...] * lk1_ref[...])) - jnp.exp(jnp.sum(lq2_ref[...] * lk2_ref[...]))
               + lam_init)
        o = _rms_norm(o0 - lam * o1, subg_ref[...]) * (1.0 - lam_init)
    else:
        o = jnp.where(low, o0, o1)
    o_ref[0] = o.astype(o_ref.dtype)


def _attention(mode, q, k, v, extra=(), lam_init=0.0):
    bsz, _, s_len, _ = q.shape
    t = min(ATTN_TILE, s_len)
    nk = s_len // t
    n_grp = v.shape[1]
    per = 2 if mode == "mla" else 1
    k5 = k.reshape(bsz, k.shape[1], nk, t, LANES)
    v5 = v.reshape(bsz, n_grp, nk, t, 2 * LANES)
    in_specs = [
        pl.BlockSpec((1, per, t, LANES), lambda b, g, i: (b, g, i, 0)),
        pl.BlockSpec((1, per, nk, t, LANES), lambda b, g, i: (b, g, 0, 0, 0)),
        pl.BlockSpec((1, 1, nk, t, 2 * LANES), lambda b, g, i: (b, g, 0, 0, 0)),
    ]
    args = [q, k5, v5]
    if mode == "fox":
        (neg_f,) = extra
        args.append(neg_f.reshape(bsz, 2 * n_grp, nk, 1, t))
        in_specs.append(pl.BlockSpec((1, 2, nk, 1, t), lambda b, g, i: (b, g, 0, 0, 0)))
    elif mode == "diff":
        bias, lq1, lk1, lq2, lk2, sub_g = extra
        args += [bias, lq1.reshape(1, -1), lk1.reshape(1, -1), lq2.reshape(1, -1), lk2.reshape(1, -1),
                 sub_g.reshape(1, -1)]
        in_specs.append(pl.BlockSpec((1, LANES, 2 * LANES), lambda b, g, i: (g, 0, 0)))
        in_specs += [_const_spec((1, DH_B))] * 4 + [_const_spec((1, 2 * DH_B))]
    scratch = [pltpu.VMEM((2, t, 1), F32), pltpu.VMEM((2, t, 2 * LANES), F32),
               pltpu.VMEM((2, t, t), F32), pltpu.VMEM((2, t, t), BF16), pltpu.VMEM((2, t, LANES), F32)]
    return pl.pallas_call(
        functools.partial(_attn_kernel, mode=mode, lam_init=lam_init),
        grid=(bsz, n_grp, s_len // t),
        in_specs=in_specs,
        out_specs=pl.BlockSpec((1, t, LANES), lambda b, g, i: (b, i, g)),
        out_shape=jax.ShapeDtypeStruct((bsz, s_len, n_grp * LANES), BF16),
        scratch_shapes=scratch,
        compiler_params=_cparams("parallel", "parallel", "arbitrary"),
        name="attn_" + mode,
    )(*args)


def _band_bias_kernel(e_ref, o_ref):
    rows, win = o_ref.shape[1], o_ref.shape[2]
    ext = jnp.broadcast_to(e_ref[0], (rows, e_ref.shape[-1]))
    o_ref[0] = pltpu.roll(ext, win + 1, 1, stride=1, stride_axis=0)[:, :win] * LOG2E


def _band_bias(rel_table):
    ext_len = BAND_WIN + BAND_GROUP
    flipped = rel_table[:, ::-1]
    left = BAND_LEAD - REL_CLIP + BAND_GROUP - 1
    ext = jnp.pad(flipped, ((0, 0), (left, ext_len - left - flipped.shape[1])), mode="edge")
    return pl.pallas_call(
        _band_bias_kernel,
        grid=(H_D,),
        in_specs=[pl.BlockSpec((1, 1, ext_len), lambda h: (h, 0, 0))],
        out_specs=pl.BlockSpec((1, BAND_GROUP, BAND_WIN), lambda h: (h, 0, 0)),
        out_shape=jax.ShapeDtypeStruct((H_D, BAND_GROUP, BAND_WIN), F32),
        compiler_params=_cparams("parallel"),
        name="band_bias",
    )(ext.reshape(H_D, 1, ext_len))


def _band_mask_kernel(o_ref):
    row = lax.broadcasted_iota(jnp.int32, o_ref.shape, 0) % BAND_GROUP
    col = lax.broadcasted_iota(jnp.int32, o_ref.shape, 1)
    first = (row // CHUNK) * CHUNK
    in_band = (col >= first) & (col < first + BAND_LEAD + CHUNK)
    o_ref[...] = jnp.where(in_band, col, -BAND_WIN * 2)


def _band_mask():
    shape = (2 * BAND_GROUP, BAND_WIN)
    return pl.pallas_call(
        _band_mask_kernel,
        out_specs=pl.BlockSpec(shape, lambda: (0, 0)),
        out_shape=jax.ShapeDtypeStruct(shape, jnp.int32),
        compiler_params=pltpu.CompilerParams(vmem_limit_bytes=VMEM_LIMIT),
        name="band_mask",
    )()


def _band_kernel(q_ref, kp_ref, kc_ref, vp_ref, vc_ref, bias_ref, vis_ref, o_ref):
    t = o_ref.shape[1]
    i = pl.program_id(2)
    lane = lax.broadcasted_iota(jnp.int32, (BAND_GROUP, LANES), 1)
    low = lane < LANES // 2
    kw = jnp.concatenate([kp_ref[0, 0], kc_ref[0, 0]], axis=0)
    vw = jnp.concatenate([vp_ref[0, 0], vc_ref[0, 0]], axis=0)
    for r in range(t // BAND_GROUP):
        q = q_ref[0, 0, r * BAND_GROUP:(r + 1) * BAND_GROUP]
        zero = jnp.zeros_like(q)
        q2 = jnp.concatenate([jnp.where(low, q, zero), jnp.where(low, zero, q)], axis=0)
        w0 = t - BAND_LEAD + r * BAND_GROUP
        keep = vis_ref[...] >= jnp.maximum(BAND_LEAD - r * BAND_GROUP - i * t, 0)
        sc = _dot_nt(q2, kw[w0:w0 + BAND_WIN]) + bias_ref[0]
        sc = jnp.where(keep, sc, NEG_INF)
        p = jnp.exp2((sc - jnp.max(sc, axis=1, keepdims=True)).astype(BF16))
        o2 = _dot(p, vw[w0:w0 + BAND_WIN])
        o2 = o2[:, :LANES] / o2[:, LANES:LANES + 1]
        o = jnp.where(low, o2[:BAND_GROUP], o2[BAND_GROUP:])
        o_ref[0, r * BAND_GROUP:(r + 1) * BAND_GROUP] = o.astype(o_ref.dtype)


def _band_attention(q, k, v, bias):
    bsz, n_grp, s_len, _ = q.shape
    t = min(BAND_TILE, s_len)
    assert t >= BAND_LEAD
    cur = lambda b, g, i: (b, g, i, 0)
    prev = lambda b, g, i: (b, g, jnp.maximum(i - 1, 0), 0)
    blk = (1, 1, t, LANES)
    vblk = (1, 1, t, 2 * LANES)
    return pl.pallas_call(
        _band_kernel,
        grid=(bsz, n_grp, s_len // t),
        in_specs=[pl.BlockSpec(blk, cur), pl.BlockSpec(blk, prev), pl.BlockSpec(blk, cur),
                  pl.BlockSpec(vblk, prev), pl.BlockSpec(vblk, cur),
                  pl.BlockSpec((1, 2 * BAND_GROUP, BAND_WIN), lambda b, g, i: (g, 0, 0)),
                  _const_spec((2 * BAND_GROUP, BAND_WIN))],
        out_specs=pl.BlockSpec((1, t, LANES), lambda b, g, i: (b, i, g)),
        out_shape=jax.ShapeDtypeStruct((bsz, s_len, n_grp * LANES), BF16),
        compiler_params=_cparams("parallel", "parallel", "parallel"),
        name="attn_band",
    )(q, k, k, v, v, bias.reshape(n_grp, 2 * BAND_GROUP, BAND_WIN), _band_mask())


def _mix_ffn_kernel(oa_ref, ob_ref, wo_ref, x_ref, mod1_ref, g1_ref, b1_ref,
                    mod_ref, wg_ref, wv_ref, cw_ref, cb_ref, wd_ref, g_ref, b_ref, o_ref, carry_ref):
    d = x_ref.shape[-1]
    tm = x_ref.shape[1]
    half = oa_ref.shape[-1]
    y = _dot(oa_ref[0], wo_ref[:half]) + _dot(ob_ref[0], wo_ref[half:])
    x = _layer_norm(DEEPNORM_ALPHA * x_ref[0] + mod1_ref[0][:, 2 * d:] * y, g1_ref[...], b1_ref[...])
    u, gate = _modulate(x, mod_ref[0], d)
    u = u.astype(BF16)
    d_ff = wg_ref.shape[1]
    n_tiles = d_ff // MXU_COLS
    edges = [(c * n_tiles // FFN_COL_CHUNKS) * MXU_COLS for c in range(FFN_COL_CHUNKS)] + [d_ff]
    seq_start = pl.program_id(1) == 0
    y = None
    for c in range(FFN_COL_CHUNKS):
        cols = slice(edges[c], edges[c + 1])
        row = lax.broadcasted_iota(jnp.int32, (tm, edges[c + 1] - edges[c]), 0)
        g = _dot(u, wg_ref[:, cols])
        val = _dot(u, wv_ref[:, cols])
        prev = jnp.where(seq_start, 0.0, carry_ref[:, cols])
        g1 = jnp.where(row == 0, prev[7:8], pltpu.roll(g, 1, 0))
        g2 = jnp.where(row == 0, prev[6:7], jnp.where(row == 1, prev[7:8], pltpu.roll(g, 2, 0)))
        carry_ref[:, cols] = g[tm - 8:]
        cw = cw_ref[:, cols]
        gc = cw[0:1] * g2 + cw[1:2] * g1 + cw[2:3] * g + cb_ref[:, cols]
        hmid = (gc * jax.nn.sigmoid(gc) * val).astype(BF16)
        yc = _dot(hmid, wd_ref[cols, :])
        y = yc if y is None else y + yc
    o_ref[0] = _layer_norm(DEEPNORM_ALPHA * x + gate * y, g_ref[...], b_ref[...])


def _mix_ffn(oa, ob, w_out, x, mod_mix, ln_g_mix, ln_b_mix,
             mod, w_gate, w_val, conv_w, conv_b, w_down, ln_g, ln_b):
    bsz, s_len, d = x.shape
    d_ff = w_gate.shape[1]
    tm = min(FFN_ROW_TILE, s_len)
    row = lambda b, t: (b, t, 0)
    per_batch = pl.BlockSpec((1, 1, 3 * d), lambda b, t: (b, 0, 0))
    once = lambda shape: pl.BlockSpec(shape, lambda b, t: (0,) * len(shape), pipeline_mode=pl.Buffered(1))
    return pl.pallas_call(
        _mix_ffn_kernel,
        grid=(bsz, s_len // tm),
        in_specs=[
            pl.BlockSpec((1, tm, oa.shape[-1]), row), pl.BlockSpec((1, tm, ob.shape[-1]), row),
            once(w_out.shape),
            pl.BlockSpec((1, tm, d), row),
            per_batch, _const_spec((1, d)), _const_spec((1, d)),
            per_batch,
            once((d, d_ff)), once((d, d_ff)),
            _const_spec((CONV_W, d_ff)), _const_spec((1, d_ff)),
            once((d_ff, d)),
            _const_spec((1, d)), _const_spec((1, d)),
        ],
        out_specs=pl.BlockSpec((1, tm, d), row),
        out_shape=jax.ShapeDtypeStruct((bsz, s_len, d), F32),
        scratch_shapes=[pltpu.VMEM((8, d_ff), F32)],
        compiler_params=_cparams("parallel", "arbitrary"),
        name="mix_ffn",
    )(oa, ob, w_out.astype(BF16), x, mod_mix, ln_g_mix.reshape(1, d), ln_b_mix.reshape(1, d),
      mod, w_gate.astype(BF16), w_val.astype(BF16), conv_w, conv_b.reshape(1, d_ff),
      w_down.astype(BF16), ln_g.reshape(1, d), ln_b.reshape(1, d))


def kernel(x, c, ada_w, ada_b, ln_g, ln_b, t5_table, ab_w_in, mla_q_norm, mla_w_uq, mla_kv_norm, mla_w_ukv, diff_lq1, diff_lk1, diff_lq2, diff_lk2, diff_sub_g, ab_w_out, cd_w_in, fox_b_f, chunk_rel_table, cd_w_out, ffn_w_gate, ffn_w_val, ffn_conv_w, ffn_conv_b, ffn_w_down):
    bsz, s_len, d = x.shape
    depth = ada_w.shape[0]
    mods = _adaln(c, ada_w, ada_b).reshape(depth, 2, bsz, 1, 3 * d)
    for i in range(depth):
        mod = mods[i, 0]
        if i % 2 == 0:
            e = i // 2
            qa, ka, va, qb, kb, vb = _proj_ab(x, mod, ab_w_in[e], mla_q_norm[e], mla_w_uq[e],
                                              mla_kv_norm[e], mla_w_ukv[e])
            o_first = _attention("mla", qa, ka, va)
            lam_init = 0.8 - 0.6 * math.exp(-0.3 * i)
            o_second = _attention("diff", qb, kb, vb,
                                  extra=(_t5_bias(t5_table), diff_lq1[e], diff_lk1[e], diff_lq2[e],
                                         diff_lk2[e], diff_sub_g[e]),
                                  lam_init=lam_init)
            w_out = ab_w_out[e]
        else:
            o = i // 2
            qc, kc, vc, qd, kd, vd, fl_t = _proj_cd(x, mod, cd_w_in[o])
            neg_f = _fox_scan(fl_t, fox_b_f[o])
            o_first = _attention("fox", qc, kc, vc, extra=(neg_f,))
            o_second = _band_attention(qd, kd, vd, _band_bias(chunk_rel_table[o]))
            w_out = cd_w_out[o]
        x = _mix_ffn(o_first, o_second, w_out, x, mod, ln_g[i, 0], ln_b[i, 0],
                     mods[i, 1], ffn_w_gate[i], ffn_w_val[i], ffn_conv_w[i], ffn_conv_b[i], ffn_w_down[i],
                     ln_g[i, 1], ln_b[i, 1])
    return x
```
